```python
import jax, jax.numpy as jnp
from jax import lax
import numpy as np

D_MODEL = 4096
BATCH = 2
SEQ = 8192
DEPTH = 1
DEC_BATCH = 4
DEC_SEQ = 2048
PAST_LEN = 128

D_MIX = D_MODEL
D_A = D_MIX // 2
HEAD_DIM_A = 128
N_HEADS_A = D_A // HEAD_DIM_A
CHUNK = 128
D_B = D_MIX - D_A
POOL_WINDOWS = (2, 4, 8, 16)
N_POOL_GROUPS = len(POOL_WINDOWS)
D_POOL_GROUP = D_B // N_POOL_GROUPS
D_IN = 2 * D_A + D_B
D_FF = ((8 * D_MODEL // 3 + 255) // 256) * 256
EPS = 1e-6
FFN_RESIDUAL = 0.5

kernel_name = "hybrid_gmlp_pool_encoder"


def rmsnorm(x, g):
    xf = x.astype(jnp.float32)
    ms = jnp.mean(xf * xf, axis=-1, keepdims=True)
    return (xf * lax.rsqrt(ms + EPS) * g.astype(jnp.float32)).astype(x.dtype)


def swiglu(x, w_gate, w_up, w_down):
    return (jax.nn.silu(x @ w_gate) * (x @ w_up)) @ w_down


def spatial_gating(z, w_spatial, b_spatial, v_norm):
    b, s, _ = z.shape
    z = jax.nn.gelu(z, approximate=False)
    u, v = jnp.split(z, 2, axis=-1)
    v = rmsnorm(v.reshape(b, s, N_HEADS_A, HEAD_DIM_A), v_norm)
    v = v.reshape(b, s // CHUNK, CHUNK, N_HEADS_A, HEAD_DIM_A)
    sv = jnp.einsum('hts,bcshd->bcthd', w_spatial, v) + b_spatial.T[None, None, :, :, None]
    return u * sv.reshape(b, s, D_A)


def multiscale_pool(p, w_pool, pool_scale):
    b, s, _ = p.shape
    pf = p.astype(jnp.float32)
    cs = jnp.concatenate([jnp.zeros((b, 1, D_B), jnp.float32), jnp.cumsum(pf, axis=1)], axis=1)
    t = jnp.arange(s)
    outs = []
    for g, w in enumerate(POOL_WINDOWS):
        sl = slice(g * D_POOL_GROUP, (g + 1) * D_POOL_GROUP)
        lo = jnp.maximum(t - w // 2, 0)
        hi = jnp.minimum(t + w // 2, s)
        csg = cs[..., sl]
        win_sum = jnp.take(csg, hi, axis=1) - jnp.take(csg, lo, axis=1)
        count = (hi - lo).astype(jnp.float32)[None, :, None]
        outs.append((win_sum / count - pf[..., sl]).astype(p.dtype))
    pooled = jnp.stack(outs, axis=2)
    y = jnp.einsum('bsgi,gio->bsgo', pooled, w_pool)
    return y.reshape(b, s, D_B) * pool_scale


def run_trunk(x, ffn1_pre, ffn1_post, ffn1_gate, ffn1_up, ffn1_down,
              mix_pre, mix_post, w_in, w_spatial, b_spatial, v_norm, w_pool, pool_scale, w_out,
              ffn2_pre, ffn2_post, ffn2_gate, ffn2_up, ffn2_down, final_norm):
    h = x
    for l in range(DEPTH):
        h = h + FFN_RESIDUAL * rmsnorm(
            swiglu(rmsnorm(h, ffn1_pre[l]), ffn1_gate[l], ffn1_up[l], ffn1_down[l]), ffn1_post[l])
        a = rmsnorm(h, mix_pre[l])
        z = a @ w_in[l]
        y_a = spatial_gating(z[..., :2 * D_A], w_spatial[l], b_spatial[l], v_norm[l])
        y_b = multiscale_pool(z[..., 2 * D_A:], w_pool[l], pool_scale[l])
        y = jnp.concatenate([y_a, y_b], axis=-1) @ w_out[l]
        h = h + rmsnorm(y, mix_post[l])
        h = h + FFN_RESIDUAL * rmsnorm(
            swiglu(rmsnorm(h, ffn2_pre[l]), ffn2_gate[l], ffn2_up[l], ffn2_down[l]), ffn2_post[l])
        h = rmsnorm(h, final_norm[l])
    return h


def setup_inputs(seed: int = 0) -> dict:
    key = jax.random.key(seed)
    ks = jax.random.split(key, 24)
    f32 = jnp.float32

    def nrm(k, shape, scale):
        return jax.random.normal(k, shape, f32) * scale

    def gain(k, shape):
        return jnp.ones(shape, f32) + 0.1 * jax.random.normal(k, shape, f32)

    L = DEPTH
    return {
        "x_prompt": jax.random.normal(ks[0], (BATCH, SEQ, D_MODEL), f32),
        "x_sample": jax.random.normal(ks[1], (DEC_BATCH, DEC_SEQ, D_MODEL), f32),
        "ffn1_pre": gain(ks[2], (L, D_MODEL)),
        "ffn1_post": gain(ks[3], (L, D_MODEL)),
        "ffn1_gate": nrm(ks[4], (L, D_MODEL, D_FF), D_MODEL ** -0.5),
        "ffn1_up": nrm(ks[5], (L, D_MODEL, D_FF), D_MODEL ** -0.5),
        "ffn1_down": nrm(ks[6], (L, D_FF, D_MODEL), D_FF ** -0.5),
        "mix_pre": gain(ks[7], (L, D_MODEL)),
        "mix_post": gain(ks[8], (L, D_MODEL)),
        "w_in": nrm(ks[9], (L, D_MODEL, D_IN), D_MODEL ** -0.5),
        "w_spatial": nrm(ks[10], (L, N_HEADS_A, CHUNK, CHUNK), CHUNK ** -0.5),
        "b_spatial": gain(ks[11], (L, N_HEADS_A, CHUNK)),
        "v_norm": gain(ks[12], (L, N_HEADS_A, HEAD_DIM_A)),
        "w_pool": nrm(ks[13], (L, N_POOL_GROUPS, D_POOL_GROUP, D_POOL_GROUP), D_POOL_GROUP ** -0.5),
        "pool_scale": gain(ks[14], (L, D_B)),
        "w_out": nrm(ks[15], (L, D_MIX, D_MODEL), D_MIX ** -0.5),
        "ffn2_pre": gain(ks[16], (L, D_MODEL)),
        "ffn2_post": gain(ks[17], (L, D_MODEL)),
        "ffn2_gate": nrm(ks[18], (L, D_MODEL, D_FF), D_MODEL ** -0.5),
        "ffn2_up": nrm(ks[19], (L, D_MODEL, D_FF), D_MODEL ** -0.5),
        "ffn2_down": nrm(ks[20], (L, D_FF, D_MODEL), D_FF ** -0.5),
        "final_norm": gain(ks[21], (L, D_MODEL)),
    }


def reference(x_prompt, x_sample, ffn1_pre, ffn1_post, ffn1_gate, ffn1_up, ffn1_down,
              mix_pre, mix_post, w_in, w_spatial, b_spatial, v_norm, w_pool, pool_scale, w_out,
              ffn2_pre, ffn2_post, ffn2_gate, ffn2_up, ffn2_down, final_norm):
    y_prompt = run_trunk(x_prompt, ffn1_pre, ffn1_post, ffn1_gate, ffn1_up, ffn1_down,
                         mix_pre, mix_post, w_in, w_spatial, b_spatial, v_norm, w_pool, pool_scale, w_out,
                         ffn2_pre, ffn2_post, ffn2_gate, ffn2_up, ffn2_down, final_norm)
    y_sample = run_trunk(x_sample, ffn1_pre, ffn1_post, ffn1_gate, ffn1_up, ffn1_down,
                         mix_pre, mix_post, w_in, w_spatial, b_spatial, v_norm, w_pool, pool_scale, w_out,
                         ffn2_pre, ffn2_post, ffn2_gate, ffn2_up, ffn2_down, final_norm)
    return (y_prompt, y_sample)
```

```python
import functools
import math

import jax
import jax.numpy as jnp
from jax import lax
from jax.experimental import pallas as pl
from jax.experimental.pallas import tpu as pltpu

F32 = jnp.float32
BF16 = jnp.bfloat16

D_MODEL = 4096
D_A = D_MODEL // 2
HEAD_DIM = 128
N_HEADS = D_A // HEAD_DIM
CHUNK = 128
D_B = D_MODEL - D_A
POOL_WINDOWS = (2, 4, 8, 16)
D_POOL_GROUP = D_B // len(POOL_WINDOWS)
POOL_HALO = max(POOL_WINDOWS) // 2
EPS = 1e-6
FFN_RESIDUAL = 0.5

V7X_SUBLANES = 8
V7X_VMEM_BYTES = 64 * 1024 * 1024
COMPILER_SCRATCH_BYTES = 6 * 1024 * 1024


def _nbytes(shape, dtype):
    return math.prod(shape) * jnp.dtype(dtype).itemsize


def _vmem_limit(pipelined_blocks, temporaries=()):
    need = 2 * sum(_nbytes(s, d) for s, d in pipelined_blocks)
    need += sum(_nbytes(s, d) for s, d in temporaries)
    need += COMPILER_SCRATCH_BYTES
    assert need <= V7X_VMEM_BYTES, need
    return need


def _params(semantics, vmem_bytes):
    return pltpu.CompilerParams(dimension_semantics=semantics, vmem_limit_bytes=vmem_bytes)


def _rms_scale(x):
    return lax.rsqrt(jnp.mean(x * x, axis=-1, keepdims=True) + EPS)


def _gelu(x):
    return 0.5 * x * (1.0 + lax.erf(x * math.sqrt(0.5)))


def _norm_kernel(x_ref, g_ref, o_ref):
    x = x_ref[...]
    o_ref[...] = (x * _rms_scale(x) * g_ref[...]).astype(o_ref.dtype)


def _rmsnorm_rows(x, gain, *, rows=256):
    t, d = x.shape
    blocks = [((rows, d), F32), ((rows, d), BF16)]
    return pl.pallas_call(
        _norm_kernel,
        grid=(t // rows,),
        in_specs=[pl.BlockSpec((rows, d), lambda i: (i, 0)),
                  pl.BlockSpec((1, d), lambda i: (0, 0))],
        out_specs=pl.BlockSpec((rows, d), lambda i: (i, 0)),
        out_shape=jax.ShapeDtypeStruct((t, d), BF16),
        compiler_params=_params(("parallel",), _vmem_limit(blocks, [((rows, d), F32)] * 2)),
        name="rmsnorm_rows",
    )(x, gain)


def _residual_kernel(y_ref, h_ref, gp_ref, gn_ref, *out_refs, scale, emit_residual):
    y = y_ref[...]
    h_new = h_ref[...] + scale * (y * _rms_scale(y) * gp_ref[...])
    nxt = h_new * _rms_scale(h_new) * gn_ref[...]
    if emit_residual:
        out_refs[0][...] = h_new
    out_refs[-1][...] = nxt.astype(out_refs[-1].dtype)


def _residual_norm(y, h, gain_post, gain_next, *, scale, emit_residual, next_dtype, rows=128):
    t, d = y.shape
    row_spec = pl.BlockSpec((rows, d), lambda i: (i, 0))
    gain_spec = pl.BlockSpec((1, d), lambda i: (0, 0))
    out_shape = [jax.ShapeDtypeStruct((t, d), next_dtype)]
    out_specs = [row_spec]
    blocks = [((rows, d), F32), ((rows, d), F32), ((rows, d), next_dtype)]
    if emit_residual:
        out_shape.insert(0, jax.ShapeDtypeStruct((t, d), F32))
        out_specs.insert(0, row_spec)
        blocks.append(((rows, d), F32))
    return pl.pallas_call(
        functools.partial(_residual_kernel, scale=scale, emit_residual=emit_residual),
        grid=(t // rows,),
        in_specs=[row_spec, row_spec, gain_spec, gain_spec],
        out_specs=out_specs,
        out_shape=out_shape,
        compiler_params=_params(("parallel",), _vmem_limit(blocks, [((rows, d), F32)] * 3)),
        name="residual_norm",
    )(y, h, gain_post, gain_next)


def _mm_kernel(x_ref, w_ref, o_ref):
    o_ref[...] = jnp.dot(x_ref[...], w_ref[...], preferred_element_type=F32).astype(o_ref.dtype)


def _mm_gelu_kernel(x_ref, w_ref, o_ref):
    z = jnp.dot(x_ref[...], w_ref[...], preferred_element_type=F32)
    o_ref[...] = _gelu(z).astype(o_ref.dtype)


def _mm_gelu_headnorm_kernel(x_ref, w_ref, g_ref, o_ref):
    z = _gelu(jnp.dot(x_ref[...], w_ref[...], preferred_element_type=F32))
    for hd in range(o_ref.shape[1] // HEAD_DIM):
        cols = slice(hd * HEAD_DIM, (hd + 1) * HEAD_DIM)
        zh = z[:, cols]
        o_ref[:, cols] = (zh * _rms_scale(zh) * g_ref[:, cols]).astype(o_ref.dtype)


def _ffn_up_kernel(x_ref, wg_ref, wu_ref, o_ref):
    x = x_ref[...]
    g = jnp.dot(x, wg_ref[...], preferred_element_type=F32)
    u = jnp.dot(x, wu_ref[...], preferred_element_type=F32)
    o_ref[...] = (g * jax.nn.sigmoid(g) * u).astype(o_ref.dtype)


def _panel_matmul(kernel, x, weights, col_gain, *, tm, tn, out_dtype, name):
    t, k = x.shape
    n = weights[0].shape[1]
    assert t % tm == 0 and n % tn == 0
    in_specs = [pl.BlockSpec((tm, k), lambda i, j: (i, 0))]
    in_specs += [pl.BlockSpec((k, tn), lambda i, j: (0, j)) for _ in weights]
    operands = [x, *weights]
    blocks = [((tm, k), x.dtype), ((tm, tn), out_dtype)] + [((k, tn), w.dtype) for w in weights]
    if col_gain is not None:
        in_specs.append(pl.BlockSpec((1, tn), lambda i, j: (0, j)))
        operands.append(col_gain)
    return pl.pallas_call(
        kernel,
        grid=(t // tm, n // tn),
        in_specs=in_specs,
        out_specs=pl.BlockSpec((tm, tn), lambda i, j: (i, j)),
        out_shape=jax.ShapeDtypeStruct((t, n), out_dtype),
        compiler_params=_params(("parallel", "arbitrary"),
                                _vmem_limit(blocks, [((tm, tn), F32)] * (2 * len(weights)))),
        name=name,
    )(*operands)


def _mixer_kernel(u_ref, v_ref, p_ref, pprev_ref, pnext_ref, ws_ref, bs_ref, wp_ref, ps_ref,
                  o_ref, *, tiles_per_seq, seq_len):
    tm = u_ref.shape[0]
    n_chunks = tm // CHUNK

    for hd in range(N_HEADS):
        cols = slice(hd * HEAD_DIM, (hd + 1) * HEAD_DIM)
        v_h = jnp.concatenate(
            [v_ref[c * CHUNK:(c + 1) * CHUNK, cols] for c in range(n_chunks)], axis=1)
        sv = jnp.dot(ws_ref[hd], v_h, preferred_element_type=F32)
        bias = bs_ref[hd]
        for c in range(n_chunks):
            rows = slice(c * CHUNK, (c + 1) * CHUNK)
            gate = sv[:, c * HEAD_DIM:(c + 1) * HEAD_DIM] + bias
            o_ref[rows, cols] = (u_ref[rows, cols].astype(F32) * gate).astype(o_ref.dtype)

    pos = pl.program_id(0) % tiles_per_seq
    prev = jnp.where(pos > 0, pprev_ref[...], 0.0)
    nxt = jnp.where(pos < tiles_per_seq - 1, pnext_ref[...], 0.0)
    row = lax.broadcasted_iota(jnp.int32, (tm, 1), 0) + pos * tm
    for g, w in enumerate(POOL_WINDOWS):
        half = w // 2
        cols = slice(g * D_POOL_GROUP, (g + 1) * D_POOL_GROUP)
        p = p_ref[:, cols]
        s = jnp.concatenate([prev[:, cols], p, nxt[:, cols]], axis=0)
        step = 1
        while step < w:
            s = s[:s.shape[0] - step] + s[step:]
            step *= 2
        win = s[POOL_HALO - half:POOL_HALO - half + tm]
        count = jnp.minimum(row + half, seq_len) - jnp.maximum(row - half, 0)
        pooled = win * (1.0 / count.astype(F32)) - p
        y_b = jnp.dot(pooled.astype(BF16), wp_ref[g], preferred_element_type=F32) * ps_ref[:, cols]
        o_ref[:, D_A + g * D_POOL_GROUP:D_A + (g + 1) * D_POOL_GROUP] = y_b.astype(o_ref.dtype)


def _mixer(u, v, p, w_spatial, b_spatial, w_pool, pool_scale, *, seq_len, tm=512):
    t = u.shape[0]
    assert seq_len % tm == 0 and tm % CHUNK == 0
    halo_blocks_per_tile = tm // POOL_HALO
    n_halo_blocks = t // POOL_HALO
    row_spec = lambda width: pl.BlockSpec((tm, width), lambda i: (i, 0))
    const3 = lambda shape: pl.BlockSpec(shape, lambda i: (0, 0, 0))
    blocks = [((tm, D_A), BF16)] * 2 + [((tm, D_B), F32), ((tm, D_MODEL), BF16),
              (w_spatial.shape, BF16), (b_spatial.shape, F32), (w_pool.shape, BF16)]
    return pl.pallas_call(
        functools.partial(_mixer_kernel, tiles_per_seq=seq_len // tm, seq_len=seq_len),
        grid=(t // tm,),
        in_specs=[
            row_spec(D_A), row_spec(D_A), row_spec(D_B),
            pl.BlockSpec((POOL_HALO, D_B),
                         lambda i: (jnp.maximum(i * halo_blocks_per_tile - 1, 0), 0)),
            pl.BlockSpec((POOL_HALO, D_B),
                         lambda i: (jnp.minimum((i + 1) * halo_blocks_per_tile, n_halo_blocks - 1), 0)),
            const3(w_spatial.shape), const3(b_spatial.shape), const3(w_pool.shape),
            pl.BlockSpec((1, D_B), lambda i: (0, 0)),
        ],
        out_specs=row_spec(D_MODEL),
        out_shape=jax.ShapeDtypeStruct((t, D_MODEL), BF16),
        compiler_params=_params(("parallel",),
                                _vmem_limit(blocks, [((tm + 2 * POOL_HALO, D_B), F32)] * 3)),
        name="token_mixer",
    )(u, v, p, p, p, w_spatial, b_spatial, w_pool, pool_scale)


def _swiglu_ffn(xn, w_gate, w_up, w_down):
    act = _panel_matmul(_ffn_up_kernel, xn, [w_gate, w_up], None,
                        tm=1024, tn=256, out_dtype=BF16, name="ffn_up")
    return _panel_matmul(_mm_kernel, act, [w_down], None,
                         tm=512, tn=256, out_dtype=F32, name="ffn_down")


def _trunk(x, w):
    b, s, d = x.shape
    h0 = x.reshape(b * s, d)
    mm_in = functools.partial(_panel_matmul, tm=1024, tn=512)

    xn = _rmsnorm_rows(h0, w["ffn1_pre"])
    y = _swiglu_ffn(xn, w["ffn1_gate"], w["ffn1_up"], w["ffn1_down"])
    h1, a = _residual_norm(y, h0, w["ffn1_post"], w["mix_pre"], scale=FFN_RESIDUAL,
                           emit_residual=True, next_dtype=BF16)

    u = mm_in(_mm_gelu_kernel, a, [w["w_in_u"]], None, out_dtype=BF16, name="w_in_u")
    v = mm_in(_mm_gelu_headnorm_kernel, a, [w["w_in_v"]], w["v_norm"], out_dtype=BF16, name="w_in_v")
    p = mm_in(_mm_kernel, a, [w["w_in_p"]], None, out_dtype=F32, name="w_in_p")
    y_cat = _mixer(u, v, p, w["w_spatial"], w["b_spatial"], w["w_pool"], w["pool_scale"], seq_len=s)
    y = mm_in(_mm_kernel, y_cat, [w["w_out"]], None, out_dtype=F32, name="w_out")
    h2, xn = _residual_norm(y, h1, w["mix_post"], w["ffn2_pre"], scale=1.0,
                            emit_residual=True, next_dtype=BF16)

    y = _swiglu_ffn(xn, w["ffn2_gate"], w["ffn2_up"], w["ffn2_down"])
    (out,) = _residual_norm(y, h2, w["ffn2_post"], w["final_norm"], scale=FFN_RESIDUAL,
                            emit_residual=False, next_dtype=F32)
    return out.reshape(b, s, d)


def kernel(x_prompt, x_sample, ffn1_pre, ffn1_post, ffn1_gate, ffn1_up, ffn1_down, mix_pre, mix_post, w_in, w_spatial, b_spatial, v_norm, w_pool, pool_scale, w_out, ffn2_pre, ffn2_post, ffn2_gate, ffn2_up, ffn2_down, final_norm):
    assert ffn1_pre.shape[0] == 1, "single-layer trunk"
    row = lambda g: g[0].reshape(1, -1).astype(F32)
    mxu = lambda m: m[0].astype(BF16)
    w = {
        "ffn1_pre": row(ffn1_pre), "ffn1_post": row(ffn1_post),
        "ffn1_gate": mxu(ffn1_gate), "ffn1_up": mxu(ffn1_up), "ffn1_down": mxu(ffn1_down),
        "mix_pre": row(mix_pre), "mix_post": row(mix_post),
        "w_in_u": mxu(w_in[..., :D_A]), "w_in_v": mxu(w_in[..., D_A:2 * D_A]),
        "w_in_p": mxu(w_in[..., 2 * D_A:]),
        "w_spatial": mxu(w_spatial),
        "b_spatial": jnp.broadcast_to(b_spatial[0][:, :, None], (N_HEADS, CHUNK, HEAD_DIM)).astype(F32),
        "v_norm": row(v_norm), "w_pool": mxu(w_pool), "pool_scale": row(pool_scale),
        "w_out": mxu(w_out),
        "ffn2_pre": row(ffn2_pre), "ffn2_post": row(ffn2_post),
        "ffn2_gate": mxu(ffn2_gate), "ffn2_up": mxu(ffn2_up), "ffn2_down": mxu(ffn2_down),
        "final_norm": row(final_norm),
    }
    return _trunk(x_prompt, w), _trunk(x_sample, w)
```

```python
import functools
import math

import jax
import jax.numpy as jnp
from jax import lax
from jax.experimental import pallas as pl
from jax.experimental.pallas import tpu as pltpu

F32 = jnp.float32
BF16 = jnp.bfloat16

D_MODEL = 4096
D_A = D_MODEL // 2
HEAD_DIM = 128
N_HEADS = D_A // HEAD_DIM
CHUNK = 128
D_B = D_MODEL - D_A
POOL_WINDOWS = (2, 4, 8, 16)
D_POOL_GROUP = D_B // len(POOL_WINDOWS)
POOL_HALO = max(POOL_WINDOWS) // 2
EPS = 1e-6
FFN_RESIDUAL = 0.5

V7X_LANES = 128
V7X_VMEM_BYTES = 64 * 1024 * 1024
COMPILER_SCRATCH_BYTES = 4 * 1024 * 1024


def _nbytes(shape, dtype):
    return math.prod(shape) * jnp.dtype(dtype).itemsize


def _vmem_limit(pipelined_blocks, temporaries=()):
    need = 2 * sum(_nbytes(s, d) for s, d in pipelined_blocks)
    need += sum(_nbytes(s, d) for s, d in temporaries)
    need += COMPILER_SCRATCH_BYTES
    assert need <= V7X_VMEM_BYTES, need
    return need


def _params(semantics, vmem_bytes):
    return pltpu.CompilerParams(dimension_semantics=semantics, vmem_limit_bytes=vmem_bytes)


def _rms_scale(x):
    return lax.rsqrt(jnp.mean(x * x, axis=-1, keepdims=True) + EPS)


def _gelu(x):
    return 0.5 * x * (1.0 + lax.erf(x * math.sqrt(0.5)))


def _lane_fold(x):
    acc = x[:, :V7X_LANES]
    for c in range(1, x.shape[1] // V7X_LANES):
        acc = acc + x[:, c * V7X_LANES:(c + 1) * V7X_LANES]
    return acc


def _norm_kernel(x_ref, g_ref, o_ref):
    x = x_ref[...]
    o_ref[...] = (x * _rms_scale(x) * g_ref[...]).astype(o_ref.dtype)


def _rmsnorm_rows(x, gain, *, rows=256):
    t, d = x.shape
    blocks = [((rows, d), F32), ((rows, d), BF16)]
    return pl.pallas_call(
        _norm_kernel,
        grid=(t // rows,),
        in_specs=[pl.BlockSpec((rows, d), lambda i: (i, 0)),
                  pl.BlockSpec((1, d), lambda i: (0, 0))],
        out_specs=pl.BlockSpec((rows, d), lambda i: (i, 0)),
        out_shape=jax.ShapeDtypeStruct((t, d), BF16),
        compiler_params=_params(("parallel",), _vmem_limit(blocks, [((rows, d), F32)] * 2)),
        name="rmsnorm_rows",
    )(x, gain)


def _ffn_up_kernel(x_ref, wg_ref, wu_ref, o_ref):
    x = x_ref[...]
    g = jnp.dot(x, wg_ref[...], preferred_element_type=F32)
    u = jnp.dot(x, wu_ref[...], preferred_element_type=F32)
    o_ref[...] = (g * jax.nn.sigmoid(g) * u).astype(o_ref.dtype)


def _ffn_up(x, w_gate, w_up, *, tm, tn):
    t, k = x.shape
    n = w_gate.shape[1]
    assert t % tm == 0 and n % tn == 0
    w_spec = pl.BlockSpec((k, tn), lambda i, j: (0, j))
    blocks = [((tm, k), x.dtype), ((k, tn), BF16), ((k, tn), BF16), ((tm, tn), BF16)]
    return pl.pallas_call(
        _ffn_up_kernel,
        grid=(t // tm, n // tn),
        in_specs=[pl.BlockSpec((tm, k), lambda i, j: (i, 0)), w_spec, w_spec],
        out_specs=pl.BlockSpec((tm, tn), lambda i, j: (i, j)),
        out_shape=jax.ShapeDtypeStruct((t, n), BF16),
        compiler_params=_params(("parallel", "arbitrary"), _vmem_limit(blocks, [((tm, tn), F32)] * 4)),
        name="ffn_up",
    )(x, w_gate, w_up)


def _w_in_kernel(x_ref, w_ref, vg_ref, u_ref, v_ref, p_ref, *, n_u, n_v):
    j = pl.program_id(1)
    z = jnp.dot(x_ref[...], w_ref[...], preferred_element_type=F32)

    @pl.when(j < n_u)
    def _():
        u_ref[...] = _gelu(z).astype(u_ref.dtype)

    @pl.when(jnp.logical_and(j >= n_u, j < n_u + n_v))
    def _():
        g = _gelu(z)
        for hd in range(v_ref.shape[1] // HEAD_DIM):
            cols = slice(hd * HEAD_DIM, (hd + 1) * HEAD_DIM)
            gh = g[:, cols]
            v_ref[:, cols] = (gh * _rms_scale(gh) * vg_ref[:, cols]).astype(v_ref.dtype)

    @pl.when(j >= n_u + n_v)
    def _():
        p_ref[...] = z


def _w_in(x, w, v_gain, *, tm, tn):
    t, k = x.shape
    assert t % tm == 0 and D_A % tn == 0 and D_B % tn == 0 and w.shape[1] == 2 * D_A + D_B
    n_u = n_v = D_A // tn
    n_p = D_B // tn
    clamp = lambda j, lo, cnt: jnp.clip(j - lo, 0, cnt - 1)
    blocks = [((tm, k), x.dtype), ((k, tn), BF16), ((tm, tn), BF16), ((tm, tn), BF16), ((tm, tn), F32)]
    return pl.pallas_call(
        functools.partial(_w_in_kernel, n_u=n_u, n_v=n_v),
        grid=(t // tm, n_u + n_v + n_p),
        in_specs=[pl.BlockSpec((tm, k), lambda i, j: (i, 0)),
                  pl.BlockSpec((k, tn), lambda i, j: (0, j)),
                  pl.BlockSpec((1, tn), lambda i, j: (0, clamp(j, n_u, n_v)))],
        out_specs=[pl.BlockSpec((tm, tn), lambda i, j: (i, clamp(j, 0, n_u))),
                   pl.BlockSpec((tm, tn), lambda i, j: (i, clamp(j, n_u, n_v))),
                   pl.BlockSpec((tm, tn), lambda i, j: (i, clamp(j, n_u + n_v, n_p)))],
        out_shape=[jax.ShapeDtypeStruct((t, D_A), BF16), jax.ShapeDtypeStruct((t, D_A), BF16),
                   jax.ShapeDtypeStruct((t, D_B), F32)],
        compiler_params=_params(("parallel", "arbitrary"), _vmem_limit(blocks, [((tm, tn), F32)] * 3)),
        name="w_in",
    )(x, w, v_gain)


def _down_kernel(x_ref, w_ref, gp_ref, gn_ref, h_hbm, *rest, scale, emit_residual):
    if emit_residual:
        hout_hbm, nxt_hbm, y_scr, ss_scr, h_scr, nxt_scr, sems = rest
    else:
        nxt_hbm, y_scr, ss_scr, h_scr, sems = rest
    i, j = pl.program_id(0), pl.program_id(1)
    n_i, n_j = pl.num_programs(0), pl.num_programs(1)
    n_chunks, tm, tn = y_scr.shape
    n_cols = h_scr.shape[1]
    panel = lambda ref, p: ref.at[pl.ds(p * tm, tm), :]

    def fetch(p):
        return pltpu.make_async_copy(panel(h_hbm, p), h_scr, sems.at[0])

    def stores(p):
        if emit_residual:
            return [pltpu.make_async_copy(h_scr, panel(hout_hbm, p), sems.at[1]),
                    pltpu.make_async_copy(nxt_scr, panel(nxt_hbm, p), sems.at[2])]
        return [pltpu.make_async_copy(h_scr, panel(nxt_hbm, p), sems.at[1])]

    y = jnp.dot(x_ref[...], w_ref[...], preferred_element_type=F32)
    y_scr[j] = y
    sq = _lane_fold(y * y)

    @pl.when(j == 0)
    def _():
        ss_scr[...] = sq

    @pl.when(j > 0)
    def _():
        ss_scr[...] += sq

    @pl.when(j == 1)
    def _():
        @pl.when(i > 0)
        def _():
            for copy in stores(i - 1):
                copy.wait()
        fetch(i).start()

    @pl.when(j == n_j - 1)
    def _():
        fetch(i).wait()
        inv_n = 1.0 / n_cols
        r1 = scale * lax.rsqrt(jnp.sum(ss_scr[...], axis=-1, keepdims=True) * inv_n + EPS)
        ss2 = jnp.zeros((tm, V7X_LANES), F32)
        for c in range(n_chunks):
            cols = slice(c * tn, (c + 1) * tn)
            h_new = h_scr[:, cols] + y_scr[c] * r1 * gp_ref[:, cols]
            h_scr[:, cols] = h_new
            ss2 = ss2 + _lane_fold(h_new * h_new)
        r2 = lax.rsqrt(jnp.sum(ss2, axis=-1, keepdims=True) * inv_n + EPS)
        for c in range(n_chunks):
            cols = slice(c * tn, (c + 1) * tn)
            nxt = h_scr[:, cols] * r2 * gn_ref[:, cols]
            if emit_residual:
                nxt_scr[:, cols] = nxt.astype(nxt_scr.dtype)
            else:
                h_scr[:, cols] = nxt
        for copy in stores(i):
            copy.start()

        @pl.when(i == n_i - 1)
        def _():
            for copy in stores(i):
                copy.wait()


def _down_residual(x, w, h, gain_post, gain_next, *, scale, emit_residual, tm, tn, name):
    t, k = x.shape
    n = w.shape[1]
    n_j = n // tn
    assert t % tm == 0 and n % tn == 0 and n_j >= 3 and h.shape == (t, n)
    any_spec = pl.BlockSpec(memory_space=pl.ANY)
    gain_spec = pl.BlockSpec((1, n), lambda i, j: (0, 0))
    out_shape = [jax.ShapeDtypeStruct((t, n), BF16 if emit_residual else F32)]
    scratch = [((n_j, tm, tn), F32), ((tm, V7X_LANES), F32), ((tm, n), F32)]
    if emit_residual:
        out_shape.insert(0, jax.ShapeDtypeStruct((t, n), F32))
        scratch.append(((tm, n), BF16))
    blocks = [((tm, k), x.dtype), ((k, tn), w.dtype), ((1, n), F32), ((1, n), F32)]
    return pl.pallas_call(
        functools.partial(_down_kernel, scale=scale, emit_residual=emit_residual),
        grid=(t // tm, n_j),
        in_specs=[pl.BlockSpec((tm, k), lambda i, j: (i, 0)),
                  pl.BlockSpec((k, tn), lambda i, j: (0, j)),
                  gain_spec, gain_spec, any_spec],
        out_specs=[any_spec] * len(out_shape),
        out_shape=out_shape,
        scratch_shapes=[pltpu.VMEM(s, d) for s, d in scratch]
        + [pltpu.SemaphoreType.DMA((1 + len(out_shape),))],
        compiler_params=_params(("arbitrary", "arbitrary"),
                                _vmem_limit(blocks, scratch + [((tm, tn), F32)] * 2)),
        name=name,
    )(x, w, gain_post, gain_next, h)


def _mixer_kernel(u_ref, v_ref, p_ref, pprev_ref, pnext_ref, ws_ref, bs_ref, wp_ref, ps_ref,
                  o_ref, *, tiles_per_seq, seq_len):
    tm = u_ref.shape[0]
    n_chunks = tm // CHUNK

    for hd in range(N_HEADS):
        cols = slice(hd * HEAD_DIM, (hd + 1) * HEAD_DIM)
        v_h = jnp.concatenate(
            [v_ref[c * CHUNK:(c + 1) * CHUNK, cols] for c in range(n_chunks)], axis=1)
        sv = jnp.dot(ws_ref[hd], v_h, preferred_element_type=F32)
        bias = bs_ref[hd]
        for c in range(n_chunks):
            rows = slice(c * CHUNK, (c + 1) * CHUNK)
            gate = sv[:, c * HEAD_DIM:(c + 1) * HEAD_DIM] + bias
            o_ref[rows, cols] = (u_ref[rows, cols].astype(F32) * gate).astype(o_ref.dtype)

    pos = pl.program_id(0) % tiles_per_seq
    prev = jnp.where(pos > 0, pprev_ref[...], 0.0)
    nxt = jnp.where(pos < tiles_per_seq - 1, pnext_ref[...], 0.0)
    row = lax.broadcasted_iota(jnp.int32, (tm, 1), 0) + pos * tm
    for g, w in enumerate(POOL_WINDOWS):
        half = w // 2
        cols = slice(g * D_POOL_GROUP, (g + 1) * D_POOL_GROUP)
        p = p_ref[:, cols]
        s = jnp.concatenate([prev[:, cols], p, nxt[:, cols]], axis=0)
        step = 1
        while step < w:
            s = s[:s.shape[0] - step] + s[step:]
            step *= 2
        win = s[POOL_HALO - half:POOL_HALO - half + tm]
        count = jnp.minimum(row + half, seq_len) - jnp.maximum(row - half, 0)
        pooled = win * (1.0 / count.astype(F32)) - p
        y_b = jnp.dot(pooled.astype(BF16), wp_ref[g], preferred_element_type=F32) * ps_ref[:, cols]
        o_ref[:, D_A + g * D_POOL_GROUP:D_A + (g + 1) * D_POOL_GROUP] = y_b.astype(o_ref.dtype)


def _mixer(u, v, p, w_spatial, b_spatial, w_pool, pool_scale, *, seq_len, tm=512):
    t = u.shape[0]
    assert seq_len % tm == 0 and tm % CHUNK == 0
    halo_blocks_per_tile = tm // POOL_HALO
    n_halo_blocks = t // POOL_HALO
    row_spec = lambda width: pl.BlockSpec((tm, width), lambda i: (i, 0))
    const3 = lambda shape: pl.BlockSpec(shape, lambda i: (0, 0, 0))
    blocks = [((tm, D_A), BF16)] * 2 + [((tm, D_B), F32), ((tm, D_MODEL), BF16),
              (w_spatial.shape, BF16), (b_spatial.shape, F32), (w_pool.shape, BF16)]
    return pl.pallas_call(
        functools.partial(_mixer_kernel, tiles_per_seq=seq_len // tm, seq_len=seq_len),
        grid=(t // tm,),
        in_specs=[
            row_spec(D_A), row_spec(D_A), row_spec(D_B),
            pl.BlockSpec((POOL_HALO, D_B),
                         lambda i: (jnp.maximum(i * halo_blocks_per_tile - 1, 0), 0)),
            pl.BlockSpec((POOL_HALO, D_B),
                         lambda i: (jnp.minimum((i + 1) * halo_blocks_per_tile, n_halo_blocks - 1), 0)),
            const3(w_spatial.shape), const3(b_spatial.shape), const3(w_pool.shape),
            pl.BlockSpec((1, D_B), lambda i: (0, 0)),
        ],
        out_specs=row_spec(D_MODEL),
        out_shape=jax.ShapeDtypeStruct((t, D_MODEL), BF16),
        compiler_params=_params(("parallel",),
                                _vmem_limit(blocks, [((tm + 2 * POOL_HALO, D_B), F32)] * 3)),
        name="token_mixer",
    )(u, v, p, p, p, w_spatial, b_spatial, w_pool, pool_scale)


def _trunk(x, w):
    b, s, d = x.shape
    h0 = x.reshape(b * s, d)
    ffn_down = functools.partial(_down_residual, tm=512, tn=256)

    xn = _rmsnorm_rows(h0, w["ffn1_pre"])
    act = _ffn_up(xn, w["ffn1_gate"], w["ffn1_up"], tm=1024, tn=256)
    h1, a = ffn_down(act, w["ffn1_down"], h0, w["ffn1_post"], w["mix_pre"],
                     scale=FFN_RESIDUAL, emit_residual=True, name="ffn1_down")

    u, v, p = _w_in(a, w["w_in"], w["v_norm"], tm=1024, tn=512)
    y_cat = _mixer(u, v, p, w["w_spatial"], w["b_spatial"], w["w_pool"], w["pool_scale"], seq_len=s)
    h2, xn = _down_residual(y_cat, w["w_out"], h1, w["mix_post"], w["ffn2_pre"],
                            scale=1.0, emit_residual=True, tm=512, tn=512, name="w_out")

    act = _ffn_up(xn, w["ffn2_gate"], w["ffn2_up"], tm=1024, tn=256)
    (out,) = ffn_down(act, w["ffn2_down"], h2, w["ffn2_post"], w["final_norm"],
                      scale=FFN_RESIDUAL, emit_residual=False, name="ffn2_down")
    return out.reshape(b, s, d)


def kernel(x_prompt, x_sample, ffn1_pre, ffn1_post, ffn1_gate, ffn1_up, ffn1_down, mix_pre, mix_post, w_in, w_spatial, b_spatial, v_norm, w_pool, pool_scale, w_out, ffn2_pre, ffn2_post, ffn2_gate, ffn2_up, ffn2_down, final_norm):
    assert ffn1_pre.shape[0] == 1, "single-layer trunk"
    row = lambda g: g[0].reshape(1, -1).astype(F32)
    mxu = lambda m: m[0].astype(BF16)
    w = {
        "ffn1_pre": row(ffn1_pre), "ffn1_post": row(ffn1_post),
        "ffn1_gate": mxu(ffn1_gate), "ffn1_up": mxu(ffn1_up), "ffn1_down": mxu(ffn1_down),
        "mix_pre": row(mix_pre), "mix_post": row(mix_post),
        "w_in": mxu(w_in),
        "w_spatial": mxu(w_spatial),
        "b_spatial": jnp.broadcast_to(b_spatial[0][:, :, None], (N_HEADS, CHUNK, HEAD_DIM)).astype(F32),
        "v_norm": row(v_norm), "w_pool": mxu(w_pool), "pool_scale": row(pool_scale),
        "w_out": mxu(w_out),
        "ffn2_pre": row(ffn2_pre), "ffn2_post": row(ffn2_post),
        "ffn2_gate": mxu(ffn2_gate), "ffn2_up": mxu(ffn2_up), "ffn2_down": mxu(ffn2_down),
        "final_norm": row(final_norm),
    }
    return _trunk(x_prompt, w), _trunk(x_sample, w)
```

```python
import functools
import math

import jax
import jax.numpy as jnp
from jax import lax
from jax.experimental import pallas as pl
from jax.experimental.pallas import tpu as pltpu

F32 = jnp.float32
BF16 = jnp.bfloat16

D_MODEL = 4096
D_A = D_MODEL // 2
HEAD_DIM = 128
N_HEADS = D_A // HEAD_DIM
CHUNK = 128
D_B = D_MODEL - D_A
POOL_WINDOWS = (2, 4, 8, 16)
D_POOL_GROUP = D_B // len(POOL_WINDOWS)
POOL_HALO = max(POOL_WINDOWS) // 2
EPS = 1e-6
FFN_RESIDUAL = 0.5

V7X_LANES = 128
V7X_VMEM_BYTES = 64 * 1024 * 1024
COMPILER_SCRATCH_BYTES = 4 * 1024 * 1024


def _nbytes(shape, dtype):
    return math.prod(shape) * jnp.dtype(dtype).itemsize


def _vmem_limit(pipelined_blocks, temporaries=()):
    need = 2 * sum(_nbytes(s, d) for s, d in pipelined_blocks)
    need += sum(_nbytes(s, d) for s, d in temporaries)
    need += COMPILER_SCRATCH_BYTES
    assert need <= V7X_VMEM_BYTES, need
    return need


def _params(semantics, vmem_bytes):
    return pltpu.CompilerParams(dimension_semantics=semantics, vmem_limit_bytes=vmem_bytes)


def _rms_scale(x):
    return lax.rsqrt(jnp.mean(x * x, axis=-1, keepdims=True) + EPS)


def _gelu(x):
    return 0.5 * x * (1.0 + lax.erf(x * math.sqrt(0.5)))


def _lane_fold(x):
    acc = x[:, :V7X_LANES]
    for c in range(1, x.shape[1] // V7X_LANES):
        acc = acc + x[:, c * V7X_LANES:(c + 1) * V7X_LANES]
    return acc


def _split_tail(n, tile):
    n_tail = n % tile
    n_main = n - n_tail
    assert n_tail % V7X_LANES == 0 and (n_tail == 0 or n_main % n_tail == 0)
    return n_main, n_tail


def _norm_kernel(x_ref, g_ref, o_ref):
    x = x_ref[...]
    o_ref[...] = (x * _rms_scale(x) * g_ref[...]).astype(o_ref.dtype)


def _rmsnorm_rows(x, gain, *, rows=256):
    t, d = x.shape
    blocks = [((rows, d), F32), ((rows, d), BF16)]
    return pl.pallas_call(
        _norm_kernel,
        grid=(t // rows,),
        in_specs=[pl.BlockSpec((rows, d), lambda i: (i, 0)),
                  pl.BlockSpec((1, d), lambda i: (0, 0))],
        out_specs=pl.BlockSpec((rows, d), lambda i: (i, 0)),
        out_shape=jax.ShapeDtypeStruct((t, d), BF16),
        compiler_params=_params(("parallel",), _vmem_limit(blocks, [((rows, d), F32)] * 2)),
        name="rmsnorm_rows",
    )(x, gain)


def _swiglu_tile(x, wg_ref, wu_ref):
    g = jnp.dot(x, wg_ref[...], preferred_element_type=F32)
    u = jnp.dot(x, wu_ref[...], preferred_element_type=F32)
    return (g * jax.nn.sigmoid(g) * u).astype(BF16)


def _ffn_up_kernel(x_ref, wg_ref, wu_ref, *rest, has_tail):
    if has_tail:
        wgt_ref, wut_ref, o_ref, ot_ref = rest
    else:
        (o_ref,) = rest
    o_ref[...] = _swiglu_tile(x_ref[...], wg_ref, wu_ref)
    if has_tail:
        @pl.when(pl.program_id(1) == 0)
        def _():
            ot_ref[...] = _swiglu_tile(x_ref[...], wgt_ref, wut_ref)


def _ffn_up(x, w_gate, w_up, *, tm, tn):
    t, k = x.shape
    n_main, n_tail = _split_tail(w_gate.shape[1], tn)
    assert t % tm == 0
    w_spec = pl.BlockSpec((k, tn), lambda i, j: (0, j))
    in_specs = [pl.BlockSpec((tm, k), lambda i, j: (i, 0)), w_spec, w_spec]
    out_specs = [pl.BlockSpec((tm, tn), lambda i, j: (i, j))]
    out_shape = [jax.ShapeDtypeStruct((t, n_main), BF16)]
    operands = [x, w_gate, w_up]
    blocks = [((tm, k), x.dtype), ((k, tn), BF16), ((k, tn), BF16), ((tm, tn), BF16)]
    if n_tail:
        tail_spec = pl.BlockSpec((k, n_tail), lambda i, j: (0, n_main // n_tail),
                                 pipeline_mode=pl.Buffered(1))
        in_specs += [tail_spec, tail_spec]
        out_specs.append(pl.BlockSpec((tm, n_tail), lambda i, j: (i, 0)))
        out_shape.append(jax.ShapeDtypeStruct((t, n_tail), BF16))
        operands += [w_gate, w_up]
        blocks += [((k, n_tail), BF16), ((tm, n_tail), BF16)]
    outs = pl.pallas_call(
        functools.partial(_ffn_up_kernel, has_tail=bool(n_tail)),
        grid=(t // tm, n_main // tn),
        in_specs=in_specs,
        out_specs=out_specs,
        out_shape=out_shape,
        compiler_params=_params(("parallel", "arbitrary"), _vmem_limit(blocks, [((tm, tn), F32)] * 4)),
        name="ffn_up",
    )(*operands)
    return outs[0], (outs[1] if n_tail else None)


def _w_in_kernel(x_ref, w_ref, vg_ref, u_ref, v_ref, p_ref, *, n_u, n_v):
    j = pl.program_id(1)
    z = jnp.dot(x_ref[...], w_ref[...], preferred_element_type=F32)

    @pl.when(j < n_u)
    def _():
        u_ref[...] = _gelu(z).astype(u_ref.dtype)

    @pl.when(jnp.logical_and(j >= n_u, j < n_u + n_v))
    def _():
        g = _gelu(z)
        for hd in range(v_ref.shape[1] // HEAD_DIM):
            cols = slice(hd * HEAD_DIM, (hd + 1) * HEAD_DIM)
            gh = g[:, cols]
            v_ref[:, cols] = (gh * _rms_scale(gh) * vg_ref[:, cols]).astype(v_ref.dtype)

    @pl.when(j >= n_u + n_v)
    def _():
        p_ref[...] = z


def _w_in(x, w, v_gain, *, tm, tn):
    t, k = x.shape
    assert t % tm == 0 and D_A % tn == 0 and D_B % tn == 0 and w.shape[1] == 2 * D_A + D_B
    n_u = n_v = D_A // tn
    n_p = D_B // tn
    clamp = lambda j, lo, cnt: jnp.clip(j - lo, 0, cnt - 1)
    blocks = [((tm, k), x.dtype), ((k, tn), BF16), ((tm, tn), BF16), ((tm, tn), BF16), ((tm, tn), F32)]
    return pl.pallas_call(
        functools.partial(_w_in_kernel, n_u=n_u, n_v=n_v),
        grid=(t // tm, n_u + n_v + n_p),
        in_specs=[pl.BlockSpec((tm, k), lambda i, j: (i, 0)),
                  pl.BlockSpec((k, tn), lambda i, j: (0, j)),
                  pl.BlockSpec((1, tn), lambda i, j: (0, clamp(j, n_u, n_v)))],
        out_specs=[pl.BlockSpec((tm, tn), lambda i, j: (i, clamp(j, 0, n_u))),
                   pl.BlockSpec((tm, tn), lambda i, j: (i, clamp(j, n_u, n_v))),
                   pl.BlockSpec((tm, tn), lambda i, j: (i, clamp(j, n_u + n_v, n_p)))],
        out_shape=[jax.ShapeDtypeStruct((t, D_A), BF16), jax.ShapeDtypeStruct((t, D_A), BF16),
                   jax.ShapeDtypeStruct((t, D_B), F32)],
        compiler_params=_params(("parallel", "arbitrary"), _vmem_limit(blocks, [((tm, tn), F32)] * 3)),
        name="w_in",
    )(x, w, v_gain)


FETCH_STEP = 2


def _down_kernel(*refs, scale, emit_residual, has_tail, chunk):
    refs = list(refs)
    xm_ref = refs.pop(0)
    xt_ref = refs.pop(0) if has_tail else None
    wm_ref = refs.pop(0)
    wt_ref = refs.pop(0) if has_tail else None
    gp_ref, gn_ref, h_hbm = refs.pop(0), refs.pop(0), refs.pop(0)
    hout_hbm = refs.pop(0) if emit_residual else None
    nxt_hbm = refs.pop(0)
    y_scr, ss_scr, h_scr = refs.pop(0), refs.pop(0), refs.pop(0)
    stage = refs.pop(0) if emit_residual else None
    (sems,) = refs

    i, k = pl.program_id(0), pl.program_id(1)
    n_i, n_k = pl.num_programs(0), pl.num_programs(1)
    tm, n_cols = y_scr.shape
    col_chunks = [slice(c * chunk, (c + 1) * chunk) for c in range(n_cols // chunk)]
    rows = lambda p: pl.ds(p * tm, tm)

    def fetch(p):
        return pltpu.make_async_copy(h_hbm.at[rows(p), :], h_scr, sems.at[0])

    def store_panel(p):
        dst = hout_hbm if emit_residual else nxt_hbm
        return pltpu.make_async_copy(h_scr, dst.at[rows(p), :], sems.at[1])

    def store_stage(p, c):
        slot = c % 2
        return pltpu.make_async_copy(stage.at[slot], nxt_hbm.at[rows(p), col_chunks[c]], sems.at[2 + slot])

    def outstanding(p):
        copies = [store_panel(p)]
        if emit_residual:
            copies += [store_stage(p, len(col_chunks) - 2), store_stage(p, len(col_chunks) - 1)]
        return copies

    def partial_y(cols):
        return jnp.dot(xm_ref[...], wm_ref[:, cols], preferred_element_type=F32)

    @pl.when(k == 0)
    def _():
        for cols in col_chunks:
            y = partial_y(cols)
            if has_tail:
                y = y + jnp.dot(xt_ref[...], wt_ref[:, cols], preferred_element_type=F32)
            y_scr[:, cols] = y

    @pl.when(jnp.logical_and(k > 0, k < n_k - 1))
    def _():
        for cols in col_chunks:
            y_scr[:, cols] += partial_y(cols)

    @pl.when(k == FETCH_STEP)
    def _():
        @pl.when(i > 0)
        def _():
            for copy in outstanding(i - 1):
                copy.wait()
        fetch(i).start()

    @pl.when(k == n_k - 1)
    def _():
        sq = jnp.zeros((tm, V7X_LANES), F32)
        for cols in col_chunks:
            y = y_scr[:, cols] + partial_y(cols)
            y_scr[:, cols] = y
            sq = sq + _lane_fold(y * y)
        ss_scr[...] = sq

    @pl.when(k == n_k - 1)
    def _():
        fetch(i).wait()
        inv_n = 1.0 / n_cols
        r1 = scale * lax.rsqrt(jnp.sum(ss_scr[...], axis=-1, keepdims=True) * inv_n + EPS)
        sq = jnp.zeros((tm, V7X_LANES), F32)
        for cols in col_chunks:
            h_new = h_scr[:, cols] + y_scr[:, cols] * r1 * gp_ref[:, cols]
            h_scr[:, cols] = h_new
            sq = sq + _lane_fold(h_new * h_new)
        r2 = lax.rsqrt(jnp.sum(sq, axis=-1, keepdims=True) * inv_n + EPS)
        if emit_residual:
            store_panel(i).start()
            for c, cols in enumerate(col_chunks):
                if c >= 2:
                    store_stage(i, c - 2).wait()
                stage[c % 2] = (h_scr[:, cols] * r2 * gn_ref[:, cols]).astype(stage.dtype)
                store_stage(i, c).start()
        else:
            for cols in col_chunks:
                h_scr[:, cols] = h_scr[:, cols] * r2 * gn_ref[:, cols]
            store_panel(i).start()

        @pl.when(i == n_i - 1)
        def _():
            for copy in outstanding(i):
                copy.wait()


def _down_residual(x_main, x_tail, w, h, gain_post, gain_next, *, scale, emit_residual,
                   tm, tk, chunk, name):
    t, k_main = x_main.shape
    k_tail = 0 if x_tail is None else x_tail.shape[1]
    n = w.shape[1]
    n_k = k_main // tk
    assert t % tm == 0 and k_main % tk == 0 and n % chunk == 0 and n // chunk >= 2
    assert n_k > FETCH_STEP + 1 and h.shape == (t, n) and w.shape[0] == k_main + k_tail
    assert k_tail == 0 or k_main % k_tail == 0

    any_spec = pl.BlockSpec(memory_space=pl.ANY)
    once = dict(pipeline_mode=pl.Buffered(1))
    gain_spec = pl.BlockSpec((1, n), lambda i, k: (0, 0), **once)
    in_specs = [pl.BlockSpec((tm, tk), lambda i, k: (i, k))]
    operands = [x_main]
    blocks = [((tm, tk), BF16), ((tk, n), BF16)]
    if k_tail:
        in_specs.append(pl.BlockSpec((tm, k_tail), lambda i, k: (i, 0)))
        operands.append(x_tail)
        blocks.append(((tm, k_tail), BF16))
    in_specs.append(pl.BlockSpec((tk, n), lambda i, k: (k, 0)))
    operands.append(w)
    if k_tail:
        in_specs.append(pl.BlockSpec((k_tail, n), lambda i, k: (k_main // k_tail, 0), **once))
        operands.append(w)
    in_specs += [gain_spec, gain_spec, any_spec]
    operands += [gain_post, gain_next, h]

    out_shape = [jax.ShapeDtypeStruct((t, n), BF16 if emit_residual else F32)]
    scratch = [((tm, n), F32), ((tm, V7X_LANES), F32), ((tm, n), F32)]
    n_sems = 2
    if emit_residual:
        out_shape.insert(0, jax.ShapeDtypeStruct((t, n), F32))
        scratch.append(((2, tm, chunk), BF16))
        n_sems = 4
    single = scratch + [((k_tail, n), BF16), ((2, n), F32), ((tm, chunk), F32), ((tm, chunk), F32)]
    return pl.pallas_call(
        functools.partial(_down_kernel, scale=scale, emit_residual=emit_residual,
                          has_tail=bool(k_tail), chunk=chunk),
        grid=(t // tm, n_k),
        in_specs=in_specs,
        out_specs=[any_spec] * len(out_shape),
        out_shape=out_shape,
        scratch_shapes=[pltpu.VMEM(s, d) for s, d in scratch] + [pltpu.SemaphoreType.DMA((n_sems,))],
        compiler_params=_params(("arbitrary", "arbitrary"), _vmem_limit(blocks, single)),
        name=name,
    )(*operands)


def _mixer_kernel(u_ref, v_ref, p_ref, pprev_ref, pnext_ref, ws_ref, bs_ref, wp_ref, ps_ref,
                  o_ref, *, tiles_per_seq, seq_len):
    tm = u_ref.shape[0]
    n_chunks = tm // CHUNK

    for hd in range(N_HEADS):
        cols = slice(hd * HEAD_DIM, (hd + 1) * HEAD_DIM)
        v_h = jnp.concatenate(
            [v_ref[c * CHUNK:(c + 1) * CHUNK, cols] for c in range(n_chunks)], axis=1)
        sv = jnp.dot(ws_ref[hd], v_h, preferred_element_type=F32)
        bias = bs_ref[hd]
        for c in range(n_chunks):
            rows = slice(c * CHUNK, (c + 1) * CHUNK)
            gate = sv[:, c * HEAD_DIM:(c + 1) * HEAD_DIM] + bias
            o_ref[rows, cols] = (u_ref[rows, cols].astype(F32) * gate).astype(o_ref.dtype)

    pos = pl.program_id(0) % tiles_per_seq
    prev = jnp.where(pos > 0, pprev_ref[...], 0.0)
    nxt = jnp.where(pos < tiles_per_seq - 1, pnext_ref[...], 0.0)
    row = lax.broadcasted_iota(jnp.int32, (tm, 1), 0) + pos * tm
    for g, w in enumerate(POOL_WINDOWS):
        half = w // 2
        cols = slice(g * D_POOL_GROUP, (g + 1) * D_POOL_GROUP)
        p = p_ref[:, cols]
        s = jnp.concatenate([prev[:, cols], p, nxt[:, cols]], axis=0)
        step = 1
        while step < w:
            s = s[:s.shape[0] - step] + s[step:]
            step *= 2
        win = s[POOL_HALO - half:POOL_HALO - half + tm]
        count = jnp.minimum(row + half, seq_len) - jnp.maximum(row - half, 0)
        pooled = win * (1.0 / count.astype(F32)) - p
        y_b = jnp.dot(pooled.astype(BF16), wp_ref[g], preferred_element_type=F32) * ps_ref[:, cols]
        o_ref[:, D_A + g * D_POOL_GROUP:D_A + (g + 1) * D_POOL_GROUP] = y_b.astype(o_ref.dtype)


def _mixer(u, v, p, w_spatial, b_spatial, w_pool, pool_scale, *, seq_len, tm=512):
    t = u.shape[0]
    assert seq_len % tm == 0 and tm % CHUNK == 0
    halo_blocks_per_tile = tm // POOL_HALO
    n_halo_blocks = t // POOL_HALO
    row_spec = lambda width: pl.BlockSpec((tm, width), lambda i: (i, 0))
    const3 = lambda shape: pl.BlockSpec(shape, lambda i: (0, 0, 0))
    blocks = [((tm, D_A), BF16)] * 2 + [((tm, D_B), F32), ((tm, D_MODEL), BF16),
              (w_spatial.shape, BF16), (b_spatial.shape, F32), (w_pool.shape, BF16)]
    return pl.pallas_call(
        functools.partial(_mixer_kernel, tiles_per_seq=seq_len // tm, seq_len=seq_len),
        grid=(t // tm,),
        in_specs=[
            row_spec(D_A), row_spec(D_A), row_spec(D_B),
            pl.BlockSpec((POOL_HALO, D_B),
                         lambda i: (jnp.maximum(i * halo_blocks_per_tile - 1, 0), 0)),
            pl.BlockSpec((POOL_HALO, D_B),
                         lambda i: (jnp.minimum((i + 1) * halo_blocks_per_tile, n_halo_blocks - 1), 0)),
            const3(w_spatial.shape), const3(b_spatial.shape), const3(w_pool.shape),
            pl.BlockSpec((1, D_B), lambda i: (0, 0)),
        ],
        out_specs=row_spec(D_MODEL),
        out_shape=jax.ShapeDtypeStruct((t, D_MODEL), BF16),
        compiler_params=_params(("parallel",),
                                _vmem_limit(blocks, [((tm + 2 * POOL_HALO, D_B), F32)] * 3)),
        name="token_mixer",
    )(u, v, p, p, p, w_spatial, b_spatial, w_pool, pool_scale)


def _trunk(x, w):
    b, s, d = x.shape
    h0 = x.reshape(b * s, d)
    ffn_up = functools.partial(_ffn_up, tm=1024, tn=512)
    down = functools.partial(_down_residual, tm=1024, tk=512, chunk=512)

    xn = _rmsnorm_rows(h0, w["ffn1_pre"])
    act, act_tail = ffn_up(xn, w["ffn1_gate"], w["ffn1_up"])
    h1, a = down(act, act_tail, w["ffn1_down"], h0, w["ffn1_post"], w["mix_pre"],
                 scale=FFN_RESIDUAL, emit_residual=True, name="ffn1_down")

    u, v, p = _w_in(a, w["w_in"], w["v_norm"], tm=1024, tn=512)
    y_cat = _mixer(u, v, p, w["w_spatial"], w["b_spatial"], w["w_pool"], w["pool_scale"], seq_len=s)
    h2, xn = down(y_cat, None, w["w_out"], h1, w["mix_post"], w["ffn2_pre"],
                  scale=1.0, emit_residual=True, name="w_out")

    act, act_tail = ffn_up(xn, w["ffn2_gate"], w["ffn2_up"])
    (out,) = down(act, act_tail, w["ffn2_down"], h2, w["ffn2_post"], w["final_norm"],
                  scale=FFN_RESIDUAL, emit_residual=False, name="ffn2_down")
    return out.reshape(b, s, d)


def kernel(x_prompt, x_sample, ffn1_pre, ffn1_post, ffn1_gate, ffn1_up, ffn1_down, mix_pre, mix_post, w_in, w_spatial, b_spatial, v_norm, w_pool, pool_scale, w_out, ffn2_pre, ffn2_post, ffn2_gate, ffn2_up, ffn2_down, final_norm):
    assert ffn1_pre.shape[0] == 1, "single-layer trunk"
    row = lambda g: g[0].reshape(1, -1).astype(F32)
    mxu = lambda m: m[0].astype(BF16)
    w = {
        "ffn1_pre": row(ffn1_pre), "ffn1_post": row(ffn1_post),
        "ffn1_gate": mxu(ffn1_gate), "ffn1_up": mxu(ffn1_up), "ffn1_down": mxu(ffn1_down),
        "mix_pre": row(mix_pre), "mix_post": row(mix_post),
        "w_in": mxu(w_in),
        "w_spatial": mxu(w_spatial),
        "b_spatial": jnp.broadcast_to(b_spatial[0][:, :, None], (N_HEADS, CHUNK, HEAD_DIM)).astype(F32),
        "v_norm": row(v_norm), "w_pool": mxu(w_pool), "pool_scale": row(pool_scale),
        "w_out": mxu(w_out),
        "ffn2_pre": row(ffn2_pre), "ffn2_post": row(ffn2_post),
        "ffn2_gate": mxu(ffn2_gate), "ffn2_up": mxu(ffn2_up), "ffn2_down": mxu(ffn2_down),
        "final_norm": row(final_norm),
    }
    return _trunk(x_prompt, w), _trunk(x_sample, w)
```

```python
import functools
import math

import jax
import jax.numpy as jnp
from jax import lax
from jax.experimental import pallas as pl
from jax.experimental.pallas import tpu as pltpu

F32 = jnp.float32
BF16 = jnp.bfloat16

D_MODEL = 4096
D_A = D_MODEL // 2
HEAD_DIM = 128
N_HEADS = D_A // HEAD_DIM
CHUNK = 128
D_B = D_MODEL - D_A
POOL_WINDOWS = (2, 4, 8, 16)
D_POOL_GROUP = D_B // len(POOL_WINDOWS)
POOL_HALO = max(POOL_WINDOWS) // 2
EPS = 1e-6
FFN_RESIDUAL = 0.5

V7X_LANES = 128
V7X_VMEM_BYTES = 64 * 1024 * 1024
COMPILER_SCRATCH_BYTES = 4 * 1024 * 1024


def _nbytes(shape, dtype):
    return math.prod(shape) * jnp.dtype(dtype).itemsize


def _vmem_limit(pipelined_blocks, temporaries=()):
    need = 2 * sum(_nbytes(s, d) for s, d in pipelined_blocks)
    need += sum(_nbytes(s, d) for s, d in temporaries)
    need += COMPILER_SCRATCH_BYTES
    assert need <= V7X_VMEM_BYTES, need
    return need


def _params(semantics, vmem_bytes):
    return pltpu.CompilerParams(dimension_semantics=semantics, vmem_limit_bytes=vmem_bytes)


def _rms_scale(x):
    return lax.rsqrt(jnp.mean(x * x, axis=-1, keepdims=True) + EPS)


def _gelu(x):
    return 0.5 * x * (1.0 + lax.erf(x * math.sqrt(0.5)))


def _lane_fold(x):
    acc = x[:, :V7X_LANES]
    for c in range(1, x.shape[1] // V7X_LANES):
        acc = acc + x[:, c * V7X_LANES:(c + 1) * V7X_LANES]
    return acc


def _split_tail(n, tile):
    n_tail = n % tile
    n_main = n - n_tail
    assert n_tail % V7X_LANES == 0 and (n_tail == 0 or n_main % n_tail == 0)
    return n_main, n_tail


def _norm_kernel(x_ref, g_ref, o_ref):
    x = x_ref[...]
    o_ref[...] = (x * _rms_scale(x) * g_ref[...]).astype(o_ref.dtype)


def _rmsnorm_rows(x, gain, *, rows=256):
    t, d = x.shape
    blocks = [((rows, d), F32), ((rows, d), BF16)]
    return pl.pallas_call(
        _norm_kernel,
        grid=(t // rows,),
        in_specs=[pl.BlockSpec((rows, d), lambda i: (i, 0)),
                  pl.BlockSpec((1, d), lambda i: (0, 0))],
        out_specs=pl.BlockSpec((rows, d), lambda i: (i, 0)),
        out_shape=jax.ShapeDtypeStruct((t, d), BF16),
        compiler_params=_params(("parallel",), _vmem_limit(blocks, [((rows, d), F32)] * 2)),
        name="rmsnorm_rows",
    )(x, gain)


def _swiglu_tile(x, wg_ref, wu_ref):
    g = jnp.dot(x, wg_ref[...], preferred_element_type=F32)
    u = jnp.dot(x, wu_ref[...], preferred_element_type=F32)
    return (g * jax.nn.sigmoid(g) * u).astype(BF16)


def _ffn_up_kernel(x_ref, wg_ref, wu_ref, *rest, has_tail):
    if has_tail:
        wgt_ref, wut_ref, o_ref, ot_ref = rest
    else:
        (o_ref,) = rest
    o_ref[...] = _swiglu_tile(x_ref[...], wg_ref, wu_ref)
    if has_tail:
        @pl.when(pl.program_id(1) == 0)
        def _():
            ot_ref[...] = _swiglu_tile(x_ref[...], wgt_ref, wut_ref)


def _ffn_up(x, w_gate, w_up, *, tm, tn):
    t, k = x.shape
    n_main, n_tail = _split_tail(w_gate.shape[1], tn)
    assert t % tm == 0
    w_spec = pl.BlockSpec((k, tn), lambda i, j: (0, j))
    in_specs = [pl.BlockSpec((tm, k), lambda i, j: (i, 0)), w_spec, w_spec]
    out_specs = [pl.BlockSpec((tm, tn), lambda i, j: (i, j))]
    out_shape = [jax.ShapeDtypeStruct((t, n_main), BF16)]
    operands = [x, w_gate, w_up]
    blocks = [((tm, k), x.dtype), ((k, tn), BF16), ((k, tn), BF16), ((tm, tn), BF16)]
    if n_tail:
        tail_spec = pl.BlockSpec((k, n_tail), lambda i, j: (0, n_main // n_tail),
                                 pipeline_mode=pl.Buffered(1))
        in_specs += [tail_spec, tail_spec]
        out_specs.append(pl.BlockSpec((tm, n_tail), lambda i, j: (i, 0)))
        out_shape.append(jax.ShapeDtypeStruct((t, n_tail), BF16))
        operands += [w_gate, w_up]
        blocks += [((k, n_tail), BF16), ((tm, n_tail), BF16)]
    outs = pl.pallas_call(
        functools.partial(_ffn_up_kernel, has_tail=bool(n_tail)),
        grid=(t // tm, n_main // tn),
        in_specs=in_specs,
        out_specs=out_specs,
        out_shape=out_shape,
        compiler_params=_params(("parallel", "arbitrary"), _vmem_limit(blocks, [((tm, tn), F32)] * 4)),
        name="ffn_up",
    )(*operands)
    return outs[0], (outs[1] if n_tail else None)


def _w_in_kernel(x_ref, w_ref, vg_ref, u_ref, v_ref, p_ref, *, n_u, n_v):
    j = pl.program_id(1)
    z = jnp.dot(x_ref[...], w_ref[...], preferred_element_type=F32)

    @pl.when(j < n_u)
    def _():
        u_ref[...] = _gelu(z).astype(u_ref.dtype)

    @pl.when(jnp.logical_and(j >= n_u, j < n_u + n_v))
    def _():
        g = _gelu(z)
        for hd in range(v_ref.shape[1] // HEAD_DIM):
            cols = slice(hd * HEAD_DIM, (hd + 1) * HEAD_DIM)
            gh = g[:, cols]
            v_ref[:, cols] = (gh * _rms_scale(gh) * vg_ref[:, cols]).astype(v_ref.dtype)

    @pl.when(j >= n_u + n_v)
    def _():
        p_ref[...] = z


def _w_in(x, w, v_gain, *, tm, tn):
    t, k = x.shape
    assert t % tm == 0 and D_A % tn == 0 and D_B % tn == 0 and w.shape[1] == 2 * D_A + D_B
    n_u = n_v = D_A // tn
    n_p = D_B // tn
    clamp = lambda j, lo, cnt: jnp.clip(j - lo, 0, cnt - 1)
    blocks = [((tm, k), x.dtype), ((k, tn), BF16), ((tm, tn), BF16), ((tm, tn), BF16), ((tm, tn), F32)]
    return pl.pallas_call(
        functools.partial(_w_in_kernel, n_u=n_u, n_v=n_v),
        grid=(t // tm, n_u + n_v + n_p),
        in_specs=[pl.BlockSpec((tm, k), lambda i, j: (i, 0)),
                  pl.BlockSpec((k, tn), lambda i, j: (0, j)),
                  pl.BlockSpec((1, tn), lambda i, j: (0, clamp(j, n_u, n_v)))],
        out_specs=[pl.BlockSpec((tm, tn), lambda i, j: (i, clamp(j, 0, n_u))),
                   pl.BlockSpec((tm, tn), lambda i, j: (i, clamp(j, n_u, n_v))),
                   pl.BlockSpec((tm, tn), lambda i, j: (i, clamp(j, n_u + n_v, n_p)))],
        out_shape=[jax.ShapeDtypeStruct((t, D_A), BF16), jax.ShapeDtypeStruct((t, D_A), BF16),
                   jax.ShapeDtypeStruct((t, D_B), F32)],
        compiler_params=_params(("parallel", "arbitrary"), _vmem_limit(blocks, [((tm, tn), F32)] * 3)),
        name="w_in",
    )(x, w, v_gain)


STAGE_SLOTS = 4
PANEL_DMA_PRIORITY = 1


def _down_kernel(*refs, scale, emit_residual, has_tail, chunk):
    refs = list(refs)
    xm_ref = refs.pop(0)
    xt_ref = refs.pop(0) if has_tail else None
    wm_ref = refs.pop(0)
    wt_ref = refs.pop(0) if has_tail else None
    gp_ref, gn_ref, h_hbm = refs.pop(0), refs.pop(0), refs.pop(0)
    hout_hbm = refs.pop(0) if emit_residual else None
    nxt_hbm = refs.pop(0)
    y_scr, ss_scr, h_scr = refs.pop(0), refs.pop(0), refs.pop(0)
    stage = refs.pop(0) if emit_residual else None
    (sems,) = refs

    i, k = pl.program_id(0), pl.program_id(1)
    n_i, n_k = pl.num_programs(0), pl.num_programs(1)
    tm, n_cols = y_scr.shape
    col_chunks = [slice(c * chunk, (c + 1) * chunk) for c in range(n_cols // chunk)]
    n_slots = stage.shape[0] if emit_residual else 0
    rows = lambda p: pl.ds(p * tm, tm)

    def fetch(p):
        return pltpu.make_async_copy(h_hbm.at[rows(p), :], h_scr, sems.at[0])

    def store_panel(p):
        dst = hout_hbm if emit_residual else nxt_hbm
        return pltpu.make_async_copy(h_scr, dst.at[rows(p), :], sems.at[1])

    def store_stage(p, c):
        slot = c % n_slots
        return pltpu.make_async_copy(stage.at[slot], nxt_hbm.at[rows(p), col_chunks[c]], sems.at[2 + slot])

    def outstanding(p):
        last = range(len(col_chunks) - n_slots, len(col_chunks))
        return [store_panel(p)] + [store_stage(p, c) for c in last]

    def partial_y(cols):
        return jnp.dot(xm_ref[...], wm_ref[:, cols], preferred_element_type=F32)

    def first_y(cols):
        y = partial_y(cols)
        if has_tail:
            y = y + jnp.dot(xt_ref[...], wt_ref[:, cols], preferred_element_type=F32)
        return y

    def row_scale(ss):
        return lax.rsqrt(jnp.sum(ss, axis=-1, keepdims=True) * (1.0 / n_cols) + EPS)

    def residual_chunk(cols, r1, sq):
        h_new = h_scr[:, cols] + y_scr[:, cols] * r1 * gp_ref[:, cols]
        h_scr[:, cols] = h_new
        return sq + _lane_fold(h_new * h_new)

    def norm_chunk(p, c, r2):
        cols = col_chunks[c]
        nxt = h_scr[:, cols] * r2 * gn_ref[:, cols]
        if emit_residual:
            if c >= n_slots:
                store_stage(p, c - n_slots).wait()
            stage[c % n_slots] = nxt.astype(stage.dtype)
            store_stage(p, c).start(priority=PANEL_DMA_PRIORITY)
        else:
            h_scr[:, cols] = nxt

    zeros = lambda: jnp.zeros((tm, V7X_LANES), F32)
    has_prev = i > 0

    @pl.when(k == 0)
    def _():
        for cols in col_chunks:
            y_scr[:, cols] = first_y(cols)

    @pl.when(jnp.logical_and(k > 0, k < n_k - 1))
    def _():
        for cols in col_chunks:
            y_scr[:, cols] += partial_y(cols)

    @pl.when(k == n_k // 2)
    def _():
        @pl.when(has_prev)
        def _():
            for copy in outstanding(i - 1):
                copy.wait()
        fetch(i).start(priority=PANEL_DMA_PRIORITY)

    @pl.when(k == n_k - 1)
    def _():
        sq = zeros()
        for cols in col_chunks:
            y = y_scr[:, cols] + partial_y(cols)
            y_scr[:, cols] = y
            sq = sq + _lane_fold(y * y)
        ss_scr[...] = sq

    @pl.when(k == n_k - 1)
    def _():
        fetch(i).wait()
        r1 = scale * row_scale(ss_scr[...])
        sq = zeros()
        for cols in col_chunks:
            sq = residual_chunk(cols, r1, sq)
        r2 = row_scale(sq)
        for c in range(len(col_chunks)):
            norm_chunk(i, c, r2)
        store_panel(i).start(priority=PANEL_DMA_PRIORITY)

        @pl.when(i == n_i - 1)
        def _():
            for copy in outstanding(i):
                copy.wait()


def _down_residual(x_main, x_tail, w, h, gain_post, gain_next, *, scale, emit_residual,
                   tm, tk, chunk, name):
    t, k_main = x_main.shape
    k_tail = 0 if x_tail is None else x_tail.shape[1]
    n = w.shape[1]
    n_k = k_main // tk
    assert t % tm == 0 and k_main % tk == 0 and n % chunk == 0 and n // chunk >= 2
    assert n_k >= 4 and h.shape == (t, n) and w.shape[0] == k_main + k_tail
    assert k_tail == 0 or k_main % k_tail == 0

    any_spec = pl.BlockSpec(memory_space=pl.ANY)
    once = dict(pipeline_mode=pl.Buffered(1))
    gain_spec = pl.BlockSpec((1, n), lambda i, k: (0, 0), **once)
    in_specs = [pl.BlockSpec((tm, tk), lambda i, k: (i, k))]
    operands = [x_main]
    blocks = [((tm, tk), BF16), ((tk, n), BF16)]
    if k_tail:
        in_specs.append(pl.BlockSpec((tm, k_tail), lambda i, k: (i, 0)))
        operands.append(x_tail)
        blocks.append(((tm, k_tail), BF16))
    in_specs.append(pl.BlockSpec((tk, n), lambda i, k: (k, 0)))
    operands.append(w)
    if k_tail:
        in_specs.append(pl.BlockSpec((k_tail, n), lambda i, k: (k_main // k_tail, 0), **once))
        operands.append(w)
    in_specs += [gain_spec, gain_spec, any_spec]
    operands += [gain_post, gain_next, h]

    out_shape = [jax.ShapeDtypeStruct((t, n), BF16 if emit_residual else F32)]
    scratch = [((tm, n), F32), ((tm, V7X_LANES), F32), ((tm, n), F32)]
    n_sems = 2
    if emit_residual:
        out_shape.insert(0, jax.ShapeDtypeStruct((t, n), F32))
        n_slots = min(STAGE_SLOTS, n // chunk)
        scratch.append(((n_slots, tm, chunk), BF16))
        n_sems = 2 + n_slots
    single = scratch + [((k_tail, n), BF16), ((2, n), F32), ((tm, chunk), F32), ((tm, chunk), F32)]
    return pl.pallas_call(
        functools.partial(_down_kernel, scale=scale, emit_residual=emit_residual,
                          has_tail=bool(k_tail), chunk=chunk),
        grid=(t // tm, n_k),
        in_specs=in_specs,
        out_specs=[any_spec] * len(out_shape),
        out_shape=out_shape,
        scratch_shapes=[pltpu.VMEM(s, d) for s, d in scratch] + [pltpu.SemaphoreType.DMA((n_sems,))],
        compiler_params=_params(("arbitrary", "arbitrary"), _vmem_limit(blocks, single)),
        name=name,
    )(*operands)


def _mixer_kernel(u_ref, v_ref, p_ref, pprev_ref, pnext_ref, ws_ref, bs_ref, wp_ref, ps_ref,
                  o_ref, *, tiles_per_seq, seq_len):
    tm = u_ref.shape[0]
    n_chunks = tm // CHUNK

    for hd in range(N_HEADS):
        cols = slice(hd * HEAD_DIM, (hd + 1) * HEAD_DIM)
        v_h = jnp.concatenate(
            [v_ref[c * CHUNK:(c + 1) * CHUNK, cols] for c in range(n_chunks)], axis=1)
        sv = jnp.dot(ws_ref[hd], v_h, preferred_element_type=F32)
        bias = bs_ref[hd]
        for c in range(n_chunks):
            rows = slice(c * CHUNK, (c + 1) * CHUNK)
            gate = sv[:, c * HEAD_DIM:(c + 1) * HEAD_DIM] + bias
            o_ref[rows, cols] = (u_ref[rows, cols].astype(F32) * gate).astype(o_ref.dtype)

    pos = pl.program_id(0) % tiles_per_seq
    prev = jnp.where(pos > 0, pprev_ref[...], 0.0)
    nxt = jnp.where(pos < tiles_per_seq - 1, pnext_ref[...], 0.0)
    row = lax.broadcasted_iota(jnp.int32, (tm, 1), 0) + pos * tm
    for g, w in enumerate(POOL_WINDOWS):
        half = w // 2
        cols = slice(g * D_POOL_GROUP, (g + 1) * D_POOL_GROUP)
        p = p_ref[:, cols]
        s = jnp.concatenate([prev[:, cols], p, nxt[:, cols]], axis=0)
        step = 1
        while step < w:
            s = s[:s.shape[0] - step] + s[step:]
            step *= 2
        win = s[POOL_HALO - half:POOL_HALO - half + tm]
        count = jnp.minimum(row + half, seq_len) - jnp.maximum(row - half, 0)
        pooled = win * (1.0 / count.astype(F32)) - p
        y_b = jnp.dot(pooled.astype(BF16), wp_ref[g], preferred_element_type=F32) * ps_ref[:, cols]
        o_ref[:, D_A + g * D_POOL_GROUP:D_A + (g + 1) * D_POOL_GROUP] = y_b.astype(o_ref.dtype)


def _mixer(u, v, p, w_spatial, b_spatial, w_pool, pool_scale, *, seq_len, tm=512):
    t = u.shape[0]
    assert seq_len % tm == 0 and tm % CHUNK == 0
    halo_blocks_per_tile = tm // POOL_HALO
    n_halo_blocks = t // POOL_HALO
    row_spec = lambda width: pl.BlockSpec((tm, width), lambda i: (i, 0))
    const3 = lambda shape: pl.BlockSpec(shape, lambda i: (0, 0, 0))
    blocks = [((tm, D_A), BF16)] * 2 + [((tm, D_B), F32), ((tm, D_MODEL), BF16),
              (w_spatial.shape, BF16), (b_spatial.shape, F32), (w_pool.shape, BF16)]
    return pl.pallas_call(
        functools.partial(_mixer_kernel, tiles_per_seq=seq_len // tm, seq_len=seq_len),
        grid=(t // tm,),
        in_specs=[
            row_spec(D_A), row_spec(D_A), row_spec(D_B),
            pl.BlockSpec((POOL_HALO, D_B),
                         lambda i: (jnp.maximum(i * halo_blocks_per_tile - 1, 0), 0)),
            pl.BlockSpec((POOL_HALO, D_B),
                         lambda i: (jnp.minimum((i + 1) * halo_blocks_per_tile, n_halo_blocks - 1), 0)),
            const3(w_spatial.shape), const3(b_spatial.shape), const3(w_pool.shape),
            pl.BlockSpec((1, D_B), lambda i: (0, 0)),
        ],
        out_specs=row_spec(D_MODEL),
        out_shape=jax.ShapeDtypeStruct((t, D_MODEL), BF16),
        compiler_params=_params(("parallel",),
                                _vmem_limit(blocks, [((tm + 2 * POOL_HALO, D_B), F32)] * 3)),
        name="token_mixer",
    )(u, v, p, p, p, w_spatial, b_spatial, w_pool, pool_scale)


def _trunk(x, w):
    b, s, d = x.shape
    h0 = x.reshape(b * s, d)
    ffn_up = functools.partial(_ffn_up, tm=1024, tn=512)
    down = functools.partial(_down_residual, tm=1024, tk=512, chunk=512)

    xn = _rmsnorm_rows(h0, w["ffn1_pre"])
    act, act_tail = ffn_up(xn, w["ffn1_gate"], w["ffn1_up"])
    h1, a = down(act, act_tail, w["ffn1_down"], h0, w["ffn1_post"], w["mix_pre"],
                 scale=FFN_RESIDUAL, emit_residual=True, name="ffn1_down")

    u, v, p = _w_in(a, w["w_in"], w["v_norm"], tm=1024, tn=512)
    y_cat = _mixer(u, v, p, w["w_spatial"], w["b_spatial"], w["w_pool"], w["pool_scale"], seq_len=s)
    h2, xn = down(y_cat, None, w["w_out"], h1, w["mix_post"], w["ffn2_pre"],
                  scale=1.0, emit_residual=True, name="w_out")

    act, act_tail = ffn_up(xn, w["ffn2_gate"], w["ffn2_up"])
    (out,) = down(act, act_tail, w["ffn2_down"], h2, w["ffn2_post"], w["final_norm"],
                  scale=FFN_RESIDUAL, emit_residual=False, name="ffn2_down")
    return out.reshape(b, s, d)


def kernel(x_prompt, x_sample, ffn1_pre, ffn1_post, ffn1_gate, ffn1_up, ffn1_down, mix_pre, mix_post, w_in, w_spatial, b_spatial, v_norm, w_pool, pool_scale, w_out, ffn2_pre, ffn2_post, ffn2_gate, ffn2_up, ffn2_down, final_norm):
    assert ffn1_pre.shape[0] == 1, "single-layer trunk"
    row = lambda g: g[0].reshape(1, -1).astype(F32)
    mxu = lambda m: m[0].astype(BF16)
    w = {
        "ffn1_pre": row(ffn1_pre), "ffn1_post": row(ffn1_post),
        "ffn1_gate": mxu(ffn1_gate), "ffn1_up": mxu(ffn1_up), "ffn1_down": mxu(ffn1_down),
        "mix_pre": row(mix_pre), "mix_post": row(mix_post),
        "w_in": mxu(w_in),
        "w_spatial": mxu(w_spatial),
        "b_spatial": jnp.broadcast_to(b_spatial[0][:, :, None], (N_HEADS, CHUNK, HEAD_DIM)).astype(F32),
        "v_norm": row(v_norm), "w_pool": mxu(w_pool), "pool_scale": row(pool_scale),
        "w_out": mxu(w_out),
        "ffn2_pre": row(ffn2_pre), "ffn2_post": row(ffn2_post),
        "ffn2_gate": mxu(ffn2_gate), "ffn2_up": mxu(ffn2_up), "ffn2_down": mxu(ffn2_down),
        "final_norm": row(final_norm),
    }
    return _trunk(x_prompt, w), _trunk(x_sample, w)
```

```python
import functools
import math

import jax
import jax.numpy as jnp
from jax import lax
from jax.experimental import pallas as pl
from jax.experimental.pallas import tpu as pltpu

F32 = jnp.float32
BF16 = jnp.bfloat16

D_MODEL = 4096
D_A = D_MODEL // 2
HEAD_DIM = 128
N_HEADS = D_A // HEAD_DIM
CHUNK = 128
D_B = D_MODEL - D_A
POOL_WINDOWS = (2, 4, 8, 16)
D_POOL_GROUP = D_B // len(POOL_WINDOWS)
POOL_HALO = max(POOL_WINDOWS) // 2
EPS = 1e-6
FFN_RESIDUAL = 0.5

V7X_LANES = 128
V7X_VMEM_BYTES = 64 * 1024 * 1024
COMPILER_SCRATCH_BYTES = 4 * 1024 * 1024


def _nbytes(shape, dtype):
    return math.prod(shape) * jnp.dtype(dtype).itemsize


def _vmem_limit(pipelined_blocks, temporaries=()):
    need = 2 * sum(_nbytes(s, d) for s, d in pipelined_blocks)
    need += sum(_nbytes(s, d) for s, d in temporaries)
    need += COMPILER_SCRATCH_BYTES
    assert need <= V7X_VMEM_BYTES, need
    return need


def _params(semantics, vmem_bytes):
    return pltpu.CompilerParams(dimension_semantics=semantics, vmem_limit_bytes=vmem_bytes)


def _rms_scale(x):
    return lax.rsqrt(jnp.mean(x * x, axis=-1, keepdims=True) + EPS)


def _gelu(x):
    return 0.5 * x * (1.0 + lax.erf(x * math.sqrt(0.5)))


def _lane_fold(x):
    acc = x[:, :V7X_LANES]
    for c in range(1, x.shape[1] // V7X_LANES):
        acc = acc + x[:, c * V7X_LANES:(c + 1) * V7X_LANES]
    return acc


def _split_tail(n, tile):
    n_tail = n % tile
    n_main = n - n_tail
    assert n_tail % V7X_LANES == 0 and (n_tail == 0 or n_main % n_tail == 0)
    return n_main, n_tail


def _norm_kernel(x_ref, g_ref, o_ref):
    x = x_ref[...]
    o_ref[...] = (x * _rms_scale(x) * g_ref[...]).astype(o_ref.dtype)


def _rmsnorm_rows(x, gain, *, rows=256):
    t, d = x.shape
    blocks = [((rows, d), F32), ((rows, d), BF16)]
    return pl.pallas_call(
        _norm_kernel,
        grid=(t // rows,),
        in_specs=[pl.BlockSpec((rows, d), lambda i: (i, 0)),
                  pl.BlockSpec((1, d), lambda i: (0, 0))],
        out_specs=pl.BlockSpec((rows, d), lambda i: (i, 0)),
        out_shape=jax.ShapeDtypeStruct((t, d), BF16),
        compiler_params=_params(("parallel",), _vmem_limit(blocks, [((rows, d), F32)] * 2)),
        name="rmsnorm_rows",
    )(x, gain)


def _swiglu_tile(x, wg_ref, wu_ref):
    g = jnp.dot(x, wg_ref[...], preferred_element_type=F32)
    u = jnp.dot(x, wu_ref[...], preferred_element_type=F32)
    return (g * jax.nn.sigmoid(g) * u).astype(BF16)


def _ffn_up_kernel(x_ref, wg_ref, wu_ref, *rest, has_tail):
    if has_tail:
        wgt_ref, wut_ref, o_ref, ot_ref = rest
    else:
        (o_ref,) = rest
    o_ref[...] = _swiglu_tile(x_ref[...], wg_ref, wu_ref)
    if has_tail:
        @pl.when(pl.program_id(1) == 0)
        def _():
            ot_ref[...] = _swiglu_tile(x_ref[...], wgt_ref, wut_ref)


def _ffn_up(x, w_gate, w_up, *, tm, tn):
    t, k = x.shape
    n_main, n_tail = _split_tail(w_gate.shape[1], tn)
    assert t % tm == 0
    w_spec = pl.BlockSpec((k, tn), lambda i, j: (0, j))
    in_specs = [pl.BlockSpec((tm, k), lambda i, j: (i, 0)), w_spec, w_spec]
    out_specs = [pl.BlockSpec((tm, tn), lambda i, j: (i, j))]
    out_shape = [jax.ShapeDtypeStruct((t, n_main), BF16)]
    operands = [x, w_gate, w_up]
    blocks = [((tm, k), x.dtype), ((k, tn), BF16), ((k, tn), BF16), ((tm, tn), BF16)]
    if n_tail:
        tail_spec = pl.BlockSpec((k, n_tail), lambda i, j: (0, n_main // n_tail),
                                 pipeline_mode=pl.Buffered(1))
        in_specs += [tail_spec, tail_spec]
        out_specs.append(pl.BlockSpec((tm, n_tail), lambda i, j: (i, 0)))
        out_shape.append(jax.ShapeDtypeStruct((t, n_tail), BF16))
        operands += [w_gate, w_up]
        blocks += [((k, n_tail), BF16), ((tm, n_tail), BF16)]
    outs = pl.pallas_call(
        functools.partial(_ffn_up_kernel, has_tail=bool(n_tail)),
        grid=(t // tm, n_main // tn),
        in_specs=in_specs,
        out_specs=out_specs,
        out_shape=out_shape,
        compiler_params=_params(("parallel", "arbitrary"), _vmem_limit(blocks, [((tm, tn), F32)] * 4)),
        name="ffn_up",
    )(*operands)
    return outs[0], (outs[1] if n_tail else None)


def _w_in_kernel(x_ref, w_ref, vg_ref, u_ref, v_ref, p_ref, *, n_u, n_v):
    j = pl.program_id(1)
    matmul = lambda: jnp.dot(x_ref[...], w_ref[...], preferred_element_type=F32)

    @pl.when(j < n_u)
    def _():
        u_ref[...] = _gelu(matmul()).astype(u_ref.dtype)

    @pl.when(jnp.logical_and(j >= n_u, j < n_u + n_v))
    def _():
        g = _gelu(matmul())
        for hd in range(v_ref.shape[1] // HEAD_DIM):
            cols = slice(hd * HEAD_DIM, (hd + 1) * HEAD_DIM)
            gh = g[:, cols]
            v_ref[:, cols] = (gh * _rms_scale(gh) * vg_ref[:, cols]).astype(v_ref.dtype)

    @pl.when(j >= n_u + n_v)
    def _():
        p_ref[...] = matmul()


def _w_in(x, w, v_gain, *, tm, tn):
    t, k = x.shape
    assert t % tm == 0 and D_A % tn == 0 and D_B % tn == 0 and w.shape[1] == 2 * D_A + D_B
    n_u = n_v = D_A // tn
    n_p = D_B // tn
    clamp = lambda j, lo, cnt: jnp.clip(j - lo, 0, cnt - 1)
    blocks = [((tm, k), x.dtype), ((k, tn), BF16), ((tm, tn), BF16), ((tm, tn), BF16), ((tm, tn), F32)]
    return pl.pallas_call(
        functools.partial(_w_in_kernel, n_u=n_u, n_v=n_v),
        grid=(t // tm, n_u + n_v + n_p),
        in_specs=[pl.BlockSpec((tm, k), lambda i, j: (i, 0)),
                  pl.BlockSpec((k, tn), lambda i, j: (0, j)),
                  pl.BlockSpec((1, tn), lambda i, j: (0, clamp(j, n_u, n_v)))],
        out_specs=[pl.BlockSpec((tm, tn), lambda i, j: (i, clamp(j, 0, n_u))),
                   pl.BlockSpec((tm, tn), lambda i, j: (i, clamp(j, n_u, n_v))),
                   pl.BlockSpec((tm, tn), lambda i, j: (i, clamp(j, n_u + n_v, n_p)))],
        out_shape=[jax.ShapeDtypeStruct((t, D_A), BF16), jax.ShapeDtypeStruct((t, D_A), BF16),
                   jax.ShapeDtypeStruct((t, D_B), F32)],
        compiler_params=_params(("parallel", "arbitrary"), _vmem_limit(blocks, [((tm, tn), F32)] * 3)),
        name="w_in",
    )(x, w, v_gain)


EPILOGUE_ROWS = 128
STAGE_SLOTS = 4
PANEL_DMA_PRIORITY = 1


def _down_kernel(*refs, scale, emit_residual, has_tail, chunk):
    refs = list(refs)
    xm_ref = refs.pop(0)
    xt_ref = refs.pop(0) if has_tail else None
    wm_ref = refs.pop(0)
    wt_ref = refs.pop(0) if has_tail else None
    gp_ref, gn_ref, h_hbm = refs.pop(0), refs.pop(0), refs.pop(0)
    hout_hbm = refs.pop(0) if emit_residual else None
    nxt_hbm = refs.pop(0)
    y_scr, ss_scr, h_scr = refs.pop(0), refs.pop(0), refs.pop(0)
    stage = refs.pop(0) if emit_residual else None
    (sems,) = refs

    i, k = pl.program_id(0), pl.program_id(1)
    n_i, n_k = pl.num_programs(0), pl.num_programs(1)
    tm, n_cols = y_scr.shape
    col_chunks = [slice(c * chunk, (c + 1) * chunk) for c in range(n_cols // chunk)]
    n_slots = stage.shape[0] if emit_residual else 0
    rows = lambda p: pl.ds(p * tm, tm)

    def fetch(p):
        return pltpu.make_async_copy(h_hbm.at[rows(p), :], h_scr, sems.at[0])

    def store_panel(p):
        dst = hout_hbm if emit_residual else nxt_hbm
        return pltpu.make_async_copy(h_scr, dst.at[rows(p), :], sems.at[1])

    rb = stage.shape[1] if emit_residual else EPILOGUE_ROWS
    row_blocks = [slice(b * rb, (b + 1) * rb) for b in range(tm // rb)]

    def store_stage(p, b):
        slot = b % n_slots
        return pltpu.make_async_copy(stage.at[slot], nxt_hbm.at[pl.ds(p * tm + b * rb, rb), :],
                                     sems.at[2 + slot])

    def outstanding(p):
        last = range(len(row_blocks) - n_slots, len(row_blocks))
        return [store_panel(p)] + [store_stage(p, b) for b in last]

    def partial_y(cols):
        return jnp.dot(xm_ref[...], wm_ref[:, cols], preferred_element_type=F32)

    def first_y(cols):
        y = partial_y(cols)
        if has_tail:
            y = y + jnp.dot(xt_ref[...], wt_ref[:, cols], preferred_element_type=F32)
        return y

    def row_scale(ss):
        return lax.rsqrt(jnp.sum(ss, axis=-1, keepdims=True) * (1.0 / n_cols) + EPS)

    def epilogue_rows(p, b):
        rws = row_blocks[b]
        r1 = scale * row_scale(ss_scr[rws, :])
        sq = jnp.zeros((rb, V7X_LANES), F32)
        for cols in col_chunks:
            h_new = h_scr[rws, cols] + y_scr[rws, cols] * r1 * gp_ref[:, cols]
            h_scr[rws, cols] = h_new
            sq = sq + _lane_fold(h_new * h_new)
        r2 = row_scale(sq)
        if emit_residual and b >= n_slots:
            store_stage(p, b - n_slots).wait()
        for cols in col_chunks:
            nxt = h_scr[rws, cols] * r2 * gn_ref[:, cols]
            if emit_residual:
                stage[b % n_slots, :, cols] = nxt.astype(stage.dtype)
            else:
                h_scr[rws, cols] = nxt
        if emit_residual:
            store_stage(p, b).start(priority=PANEL_DMA_PRIORITY)

    has_prev = i > 0

    @pl.when(k == 0)
    def _():
        for cols in col_chunks:
            y_scr[:, cols] = first_y(cols)

    @pl.when(jnp.logical_and(k > 0, k < n_k - 1))
    def _():
        for cols in col_chunks:
            y_scr[:, cols] += partial_y(cols)

    @pl.when(k == n_k // 2)
    def _():
        @pl.when(has_prev)
        def _():
            for copy in outstanding(i - 1):
                copy.wait()
        fetch(i).start(priority=PANEL_DMA_PRIORITY)

    @pl.when(k == n_k - 1)
    def _():
        sq = jnp.zeros((tm, V7X_LANES), F32)
        for cols in col_chunks:
            y = y_scr[:, cols] + partial_y(cols)
            y_scr[:, cols] = y
            sq = sq + _lane_fold(y * y)
        ss_scr[...] = sq

    @pl.when(k == n_k - 1)
    def _():
        fetch(i).wait()
        for b in range(len(row_blocks)):
            epilogue_rows(i, b)
        store_panel(i).start(priority=PANEL_DMA_PRIORITY)

        @pl.when(i == n_i - 1)
        def _():
            for copy in outstanding(i):
                copy.wait()


def _down_residual(x_main, x_tail, w, h, gain_post, gain_next, *, scale, emit_residual,
                   tm, tk, chunk, name):
    t, k_main = x_main.shape
    k_tail = 0 if x_tail is None else x_tail.shape[1]
    n = w.shape[1]
    n_k = k_main // tk
    assert t % tm == 0 and tm % EPILOGUE_ROWS == 0 and k_main % tk == 0 and n % chunk == 0
    assert n_k >= 4 and h.shape == (t, n) and w.shape[0] == k_main + k_tail
    assert k_tail == 0 or k_main % k_tail == 0

    any_spec = pl.BlockSpec(memory_space=pl.ANY)
    once = dict(pipeline_mode=pl.Buffered(1))
    gain_spec = pl.BlockSpec((1, n), lambda i, k: (0, 0), **once)
    in_specs = [pl.BlockSpec((tm, tk), lambda i, k: (i, k))]
    operands = [x_main]
    blocks = [((tm, tk), BF16), ((tk, n), BF16)]
    if k_tail:
        in_specs.append(pl.BlockSpec((tm, k_tail), lambda i, k: (i, 0)))
        operands.append(x_tail)
        blocks.append(((tm, k_tail), BF16))
    in_specs.append(pl.BlockSpec((tk, n), lambda i, k: (k, 0)))
    operands.append(w)
    if k_tail:
        in_specs.append(pl.BlockSpec((k_tail, n), lambda i, k: (k_main // k_tail, 0), **once))
        operands.append(w)
    in_specs += [gain_spec, gain_spec, any_spec]
    operands += [gain_post, gain_next, h]

    out_shape = [jax.ShapeDtypeStruct((t, n), BF16 if emit_residual else F32)]
    scratch = [((tm, n), F32), ((tm, V7X_LANES), F32), ((tm, n), F32)]
    n_sems = 2
    if emit_residual:
        out_shape.insert(0, jax.ShapeDtypeStruct((t, n), F32))
        n_slots = min(STAGE_SLOTS, tm // EPILOGUE_ROWS)
        scratch.append(((n_slots, EPILOGUE_ROWS, n), BF16))
        n_sems = 2 + n_slots
    single = scratch + [((k_tail, n), BF16), ((2, n), F32), ((tm, chunk), F32), ((tm, chunk), F32)]
    return pl.pallas_call(
        functools.partial(_down_kernel, scale=scale, emit_residual=emit_residual,
                          has_tail=bool(k_tail), chunk=chunk),
        grid=(t // tm, n_k),
        in_specs=in_specs,
        out_specs=[any_spec] * len(out_shape),
        out_shape=out_shape,
        scratch_shapes=[pltpu.VMEM(s, d) for s, d in scratch] + [pltpu.SemaphoreType.DMA((n_sems,))],
        compiler_params=_params(("arbitrary", "arbitrary"), _vmem_limit(blocks, single)),
        name=name,
    )(*operands)


def _mixer_kernel(u_ref, v_ref, p_ref, pprev_ref, pnext_ref, ws_ref, bs_ref, wp_ref, ps_ref,
                  o_ref, *, tiles_per_seq, seq_len):
    tm = u_ref.shape[0]
    n_chunks = tm // CHUNK

    for hd in range(N_HEADS):
        cols = slice(hd * HEAD_DIM, (hd + 1) * HEAD_DIM)
        v_h = jnp.concatenate(
            [v_ref[c * CHUNK:(c + 1) * CHUNK, cols] for c in range(n_chunks)], axis=1)
        sv = jnp.dot(ws_ref[hd], v_h, preferred_element_type=F32)
        bias = bs_ref[hd]
        for c in range(n_chunks):
            rows = slice(c * CHUNK, (c + 1) * CHUNK)
            gate = sv[:, c * HEAD_DIM:(c + 1) * HEAD_DIM] + bias
            o_ref[rows, cols] = (u_ref[rows, cols].astype(F32) * gate).astype(o_ref.dtype)

    pos = pl.program_id(0) % tiles_per_seq
    prev = jnp.where(pos > 0, pprev_ref[...], 0.0)
    nxt = jnp.where(pos < tiles_per_seq - 1, pnext_ref[...], 0.0)
    row = lax.broadcasted_iota(jnp.int32, (tm, 1), 0) + pos * tm
    for g, w in enumerate(POOL_WINDOWS):
        half = w // 2
        cols = slice(g * D_POOL_GROUP, (g + 1) * D_POOL_GROUP)
        p = p_ref[:, cols]
        s = jnp.concatenate([prev[:, cols], p, nxt[:, cols]], axis=0)
        step = 1
        while step < w:
            s = s[:s.shape[0] - step] + s[step:]
            step *= 2
        win = s[POOL_HALO - half:POOL_HALO - half + tm]
        count = jnp.minimum(row + half, seq_len) - jnp.maximum(row - half, 0)
        pooled = win * (1.0 / count.astype(F32)) - p
        y_b = jnp.dot(pooled.astype(BF16), wp_ref[g], preferred_element_type=F32) * ps_ref[:, cols]
        o_ref[:, D_A + g * D_POOL_GROUP:D_A + (g + 1) * D_POOL_GROUP] = y_b.astype(o_ref.dtype)


def _mixer(u, v, p, w_spatial, b_spatial, w_pool, pool_scale, *, seq_len, tm=512):
    t = u.shape[0]
    assert seq_len % tm == 0 and tm % CHUNK == 0
    halo_blocks_per_tile = tm // POOL_HALO
    n_halo_blocks = t // POOL_HALO
    row_spec = lambda width: pl.BlockSpec((tm, width), lambda i: (i, 0))
    const3 = lambda shape: pl.BlockSpec(shape, lambda i: (0, 0, 0))
    blocks = [((tm, D_A), BF16)] * 2 + [((tm, D_B), F32), ((tm, D_MODEL), BF16),
              (w_spatial.shape, BF16), (b_spatial.shape, F32), (w_pool.shape, BF16)]
    return pl.pallas_call(
        functools.partial(_mixer_kernel, tiles_per_seq=seq_len // tm, seq_len=seq_len),
        grid=(t // tm,),
        in_specs=[
            row_spec(D_A), row_spec(D_A), row_spec(D_B),
            pl.BlockSpec((POOL_HALO, D_B),
                         lambda i: (jnp.maximum(i * halo_blocks_per_tile - 1, 0), 0)),
            pl.BlockSpec((POOL_HALO, D_B),
                         lambda i: (jnp.minimum((i + 1) * halo_blocks_per_tile, n_halo_blocks - 1), 0)),
            const3(w_spatial.shape), const3(b_spatial.shape), const3(w_pool.shape),
            pl.BlockSpec((1, D_B), lambda i: (0, 0)),
        ],
        out_specs=row_spec(D_MODEL),
        out_shape=jax.ShapeDtypeStruct((t, D_MODEL), BF16),
        compiler_params=_params(("parallel",),
                                _vmem_limit(blocks, [((tm + 2 * POOL_HALO, D_B), F32)] * 3)),
        name="token_mixer",
    )(u, v, p, p, p, w_spatial, b_spatial, w_pool, pool_scale)


def _trunk(x, w):
    b, s, d = x.shape
    h0 = x.reshape(b * s, d)
    ffn_up = functools.partial(_ffn_up, tm=1024, tn=512)
    down = functools.partial(_down_residual, tm=1024, tk=512, chunk=512)

    xn = _rmsnorm_rows(h0, w["ffn1_pre"])
    act, act_tail = ffn_up(xn, w["ffn1_gate"], w["ffn1_up"])
    h1, a = down(act, act_tail, w["ffn1_down"], h0, w["ffn1_post"], w["mix_pre"],
                 scale=FFN_RESIDUAL, emit_residual=True, name="ffn1_down")

    u, v, p = _w_in(a, w["w_in"], w["v_norm"], tm=1024, tn=512)
    y_cat = _mixer(u, v, p, w["w_spatial"], w["b_spatial"], w["w_pool"], w["pool_scale"], seq_len=s)
    h2, xn = down(y_cat, None, w["w_out"], h1, w["mix_post"], w["ffn2_pre"],
                  scale=1.0, emit_residual=True, name="w_out")

    act, act_tail = ffn_up(xn, w["ffn2_gate"], w["ffn2_up"])
    (out,) = down(act, act_tail, w["ffn2_down"], h2, w["ffn2_post"], w["final_norm"],
                  scale=FFN_RESIDUAL, emit_residual=False, name="ffn2_down")
    return out.reshape(b, s, d)


def kernel(x_prompt, x_sample, ffn1_pre, ffn1_post, ffn1_gate, ffn1_up, ffn1_down, mix_pre, mix_post, w_in, w_spatial, b_spatial, v_norm, w_pool, pool_scale, w_out, ffn2_pre, ffn2_post, ffn2_gate, ffn2_up, ffn2_down, final_norm):
    assert ffn1_pre.shape[0] == 1, "single-layer trunk"
    row = lambda g: g[0].reshape(1, -1).astype(F32)
    mxu = lambda m: m[0].astype(BF16)
    w = {
        "ffn1_pre": row(ffn1_pre), "ffn1_post": row(ffn1_post),
        "ffn1_gate": mxu(ffn1_gate), "ffn1_up": mxu(ffn1_up), "ffn1_down": mxu(ffn1_down),
        "mix_pre": row(mix_pre), "mix_post": row(mix_post),
        "w_in": mxu(w_in),
        "w_spatial": mxu(w_spatial),
        "b_spatial": jnp.broadcast_to(b_spatial[0][:, :, None], (N_HEADS, CHUNK, HEAD_DIM)).astype(F32),
        "v_norm": row(v_norm), "w_pool": mxu(w_pool), "pool_scale": row(pool_scale),
        "w_out": mxu(w_out),
        "ffn2_pre": row(ffn2_pre), "ffn2_post": row(ffn2_post),
        "ffn2_gate": mxu(ffn2_gate), "ffn2_up": mxu(ffn2_up), "ffn2_down": mxu(ffn2_down),
        "final_norm": row(final_norm),
    }
    return _trunk(x_prompt, w), _trunk(x_sample, w)
```

```python
import functools
import math

import jax
import jax.numpy as jnp
from jax import lax
from jax.experimental import pallas as pl
from jax.experimental.pallas import tpu as pltpu

F32 = jnp.float32
BF16 = jnp.bfloat16

D_MODEL = 4096
D_A = D_MODEL // 2
HEAD_DIM = 128
N_HEADS = D_A // HEAD_DIM
CHUNK = 128
D_B = D_MODEL - D_A
POOL_WINDOWS = (2, 4, 8, 16)
D_POOL_GROUP = D_B // len(POOL_WINDOWS)
POOL_HALO = max(POOL_WINDOWS) // 2
EPS = 1e-6
FFN_RESIDUAL = 0.5

V7X_LANES = 128
V7X_VMEM_BYTES = 64 * 1024 * 1024
COMPILER_SCRATCH_BYTES = 4 * 1024 * 1024
PANEL_DMA_PRIORITY = 1


def _nbytes(shape, dtype):
    return math.prod(shape) * jnp.dtype(dtype).itemsize


def _vmem_limit(pipelined_blocks, temporaries=()):
    need = 2 * sum(_nbytes(s, d) for s, d in pipelined_blocks)
    need += sum(_nbytes(s, d) for s, d in temporaries)
    need += COMPILER_SCRATCH_BYTES
    assert need <= V7X_VMEM_BYTES, need
    return need


def _params(semantics, vmem_bytes):
    return pltpu.CompilerParams(dimension_semantics=semantics, vmem_limit_bytes=vmem_bytes)


def _rms_scale(x):
    return lax.rsqrt(jnp.mean(x * x, axis=-1, keepdims=True) + EPS)


def _gelu(x):
    return 0.5 * x * (1.0 + lax.erf(x * math.sqrt(0.5)))


def _lane_fold(x):
    acc = x[:, :V7X_LANES]
    for c in range(1, x.shape[1] // V7X_LANES):
        acc = acc + x[:, c * V7X_LANES:(c + 1) * V7X_LANES]
    return acc


def _split_tail(n, tile):
    n_tail = n % tile
    n_main = n - n_tail
    assert n_tail % V7X_LANES == 0 and (n_tail == 0 or n_main % n_tail == 0)
    return n_main, n_tail


def _norm_kernel(x_ref, g_ref, o_ref):
    x = x_ref[...]
    o_ref[...] = (x * _rms_scale(x) * g_ref[...]).astype(o_ref.dtype)


def _rmsnorm_rows(x, gain, *, rows=256):
    t, d = x.shape
    blocks = [((rows, d), F32), ((rows, d), BF16)]
    return pl.pallas_call(
        _norm_kernel,
        grid=(t // rows,),
        in_specs=[pl.BlockSpec((rows, d), lambda i: (i, 0)),
                  pl.BlockSpec((1, d), lambda i: (0, 0))],
        out_specs=pl.BlockSpec((rows, d), lambda i: (i, 0)),
        out_shape=jax.ShapeDtypeStruct((t, d), BF16),
        compiler_params=_params(("parallel",), _vmem_limit(blocks, [((rows, d), F32)] * 2)),
        name="rmsnorm_rows",
    )(x, gain)


def _swiglu_tile(x, wg_ref, wu_ref):
    g = jnp.dot(x, wg_ref[...], preferred_element_type=F32)
    u = jnp.dot(x, wu_ref[...], preferred_element_type=F32)
    return (g * jax.nn.sigmoid(g) * u).astype(BF16)


def _ffn_up_kernel(x_ref, wg_ref, wu_ref, *rest, has_tail):
    if has_tail:
        wgt_ref, wut_ref, o_ref, ot_ref = rest
    else:
        (o_ref,) = rest
    o_ref[...] = _swiglu_tile(x_ref[...], wg_ref, wu_ref)
    if has_tail:
        @pl.when(pl.program_id(1) == 0)
        def _():
            ot_ref[...] = _swiglu_tile(x_ref[...], wgt_ref, wut_ref)


def _ffn_up(x, w_gate, w_up, *, tm, tn):
    t, k = x.shape
    n_main, n_tail = _split_tail(w_gate.shape[1], tn)
    assert t % tm == 0
    w_spec = pl.BlockSpec((k, tn), lambda i, j: (0, j))
    in_specs = [pl.BlockSpec((tm, k), lambda i, j: (i, 0)), w_spec, w_spec]
    out_specs = [pl.BlockSpec((tm, tn), lambda i, j: (i, j))]
    out_shape = [jax.ShapeDtypeStruct((t, n_main), BF16)]
    operands = [x, w_gate, w_up]
    blocks = [((tm, k), x.dtype), ((k, tn), BF16), ((k, tn), BF16), ((tm, tn), BF16)]
    if n_tail:
        tail_spec = pl.BlockSpec((k, n_tail), lambda i, j: (0, n_main // n_tail),
                                 pipeline_mode=pl.Buffered(1))
        in_specs += [tail_spec, tail_spec]
        out_specs.append(pl.BlockSpec((tm, n_tail), lambda i, j: (i, 0)))
        out_shape.append(jax.ShapeDtypeStruct((t, n_tail), BF16))
        operands += [w_gate, w_up]
        blocks += [((k, n_tail), BF16), ((tm, n_tail), BF16)]
    outs = pl.pallas_call(
        functools.partial(_ffn_up_kernel, has_tail=bool(n_tail)),
        grid=(t // tm, n_main // tn),
        in_specs=in_specs,
        out_specs=out_specs,
        out_shape=out_shape,
        compiler_params=_params(("parallel", "arbitrary"), _vmem_limit(blocks, [((tm, tn), F32)] * 4)),
        name="ffn_up",
    )(*operands)
    return outs[0], (outs[1] if n_tail else None)


def _side_cast_step(step, srcs, dsts, f32_stage, bf16_stage, sems):
    rows = f32_stage.shape[1]
    blocks_per_array = srcs[0].shape[0] // rows
    n_blocks = len(srcs) * blocks_per_array
    block_rows = lambda blk: pl.ds((blk % blocks_per_array) * rows, rows)

    def per_array(blk, fn):
        for a, (src, dst) in enumerate(zip(srcs, dsts)):
            @pl.when(blk // blocks_per_array == a)
            def _():
                fn(src, dst)

    def load(src, blk):
        return pltpu.make_async_copy(src.at[block_rows(blk), :], f32_stage.at[blk % 2], sems.at[blk % 2])

    def store(dst, blk):
        return pltpu.make_async_copy(bf16_stage.at[blk % 2], dst.at[block_rows(blk), :], sems.at[2 + blk % 2])

    @pl.when(step == 0)
    def _():
        load(srcs[0], step).start(priority=PANEL_DMA_PRIORITY)

    @pl.when(step + 1 < n_blocks)
    def _():
        per_array(step + 1, lambda src, dst: load(src, step + 1).start(priority=PANEL_DMA_PRIORITY))

    @pl.when(step < n_blocks)
    def _():
        load(srcs[0], step).wait()

        @pl.when(step >= 2)
        def _():
            store(dsts[0], step - 2).wait()
        bf16_stage[step % 2] = f32_stage[step % 2].astype(bf16_stage.dtype)
        per_array(step, lambda src, dst: store(dst, step).start(priority=PANEL_DMA_PRIORITY))

    @pl.when(step == n_blocks - 1)
    def _():
        if n_blocks > 1:
            store(dsts[0], step - 1).wait()
        store(dsts[0], step).wait()


def _w_in_kernel(x_ref, w_ref, vg_ref, *rest, n_u, n_v, n_side):
    side_srcs, rest = rest[:n_side], rest[n_side:]
    (u_ref, v_ref, p_ref), rest = rest[:3], rest[3:]
    side_dsts, scratch = rest[:n_side], rest[n_side:]
    j = pl.program_id(1)
    if n_side:
        _side_cast_step(pl.program_id(0) * pl.num_programs(1) + j, side_srcs, side_dsts, *scratch)

    matmul = lambda: jnp.dot(x_ref[...], w_ref[...], preferred_element_type=F32)

    @pl.when(j < n_u)
    def _():
        u_ref[...] = _gelu(matmul()).astype(u_ref.dtype)

    @pl.when(jnp.logical_and(j >= n_u, j < n_u + n_v))
    def _():
        g = _gelu(matmul())
        for hd in range(v_ref.shape[1] // HEAD_DIM):
            cols = slice(hd * HEAD_DIM, (hd + 1) * HEAD_DIM)
            gh = g[:, cols]
            v_ref[:, cols] = (gh * _rms_scale(gh) * vg_ref[:, cols]).astype(v_ref.dtype)

    @pl.when(j >= n_u + n_v)
    def _():
        p_ref[...] = matmul()


def _w_in(x, w, v_gain, side=(), *, tm, tn, side_rows=None):
    t, k = x.shape
    assert t % tm == 0 and D_A % tn == 0 and D_B % tn == 0 and w.shape[1] == 2 * D_A + D_B
    n_u = n_v = D_A // tn
    n_p = D_B // tn
    grid = (t // tm, n_u + n_v + n_p)
    clamp = lambda j, lo, cnt: jnp.clip(j - lo, 0, cnt - 1)
    any_spec = pl.BlockSpec(memory_space=pl.ANY)
    blocks = [((tm, k), x.dtype), ((k, tn), BF16), ((tm, tn), BF16), ((tm, tn), BF16), ((tm, tn), F32)]
    scratch = []
    if side:
        rows, cols = side[0].shape
        assert all(s.shape == (rows, cols) and s.dtype == F32 for s in side) and rows % side_rows == 0
        assert len(side) * (rows // side_rows) <= grid[0] * grid[1]
        scratch = [((2, side_rows, cols), F32), ((2, side_rows, cols), BF16)]
    outs = pl.pallas_call(
        functools.partial(_w_in_kernel, n_u=n_u, n_v=n_v, n_side=len(side)),
        grid=grid,
        in_specs=[pl.BlockSpec((tm, k), lambda i, j: (i, 0)),
                  pl.BlockSpec((k, tn), lambda i, j: (0, j)),
                  pl.BlockSpec((1, tn), lambda i, j: (0, clamp(j, n_u, n_v)))] + [any_spec] * len(side),
        out_specs=[pl.BlockSpec((tm, tn), lambda i, j: (i, clamp(j, 0, n_u))),
                   pl.BlockSpec((tm, tn), lambda i, j: (i, clamp(j, n_u, n_v))),
                   pl.BlockSpec((tm, tn), lambda i, j: (i, clamp(j, n_u + n_v, n_p)))]
        + [any_spec] * len(side),
        out_shape=[jax.ShapeDtypeStruct((t, D_A), BF16), jax.ShapeDtypeStruct((t, D_A), BF16),
                   jax.ShapeDtypeStruct((t, D_B), F32)]
        + [jax.ShapeDtypeStruct(s.shape, BF16) for s in side],
        scratch_shapes=[pltpu.VMEM(s, d) for s, d in scratch]
        + ([pltpu.SemaphoreType.DMA((4,))] if side else []),
        compiler_params=_params(("arbitrary", "arbitrary"),
                                _vmem_limit(blocks, scratch + [((tm, tn), F32)] * 3)),
        name="w_in",
    )(x, w, v_gain, *side)
    return outs[0], outs[1], outs[2], list(outs[3:])


FFN_UP_CAST_ROWS = 64
FFN_DOWN_CAST_ROWS = 256
EPILOGUE_ROWS = 128
STAGE_SLOTS = 4


def _down_kernel(*refs, scale, emit_residual, has_tail, chunk):
    refs = list(refs)
    xm_ref = refs.pop(0)
    xt_ref = refs.pop(0) if has_tail else None
    wm_ref = refs.pop(0)
    wt_ref = refs.pop(0) if has_tail else None
    gp_ref, gn_ref, h_hbm = refs.pop(0), refs.pop(0), refs.pop(0)
    hout_hbm = refs.pop(0) if emit_residual else None
    nxt_hbm = refs.pop(0)
    y_scr, ss_scr, h_scr = refs.pop(0), refs.pop(0), refs.pop(0)
    stage = refs.pop(0) if emit_residual else None
    (sems,) = refs

    i, k = pl.program_id(0), pl.program_id(1)
    n_i, n_k = pl.num_programs(0), pl.num_programs(1)
    tm, n_cols = y_scr.shape
    col_chunks = [slice(c * chunk, (c + 1) * chunk) for c in range(n_cols // chunk)]
    n_slots = stage.shape[0] if emit_residual else 0
    rows = lambda p: pl.ds(p * tm, tm)

    def fetch(p):
        return pltpu.make_async_copy(h_hbm.at[rows(p), :], h_scr, sems.at[0])

    def store_panel(p):
        dst = hout_hbm if emit_residual else nxt_hbm
        return pltpu.make_async_copy(h_scr, dst.at[rows(p), :], sems.at[1])

    rb = stage.shape[1] if emit_residual else EPILOGUE_ROWS
    row_blocks = [slice(b * rb, (b + 1) * rb) for b in range(tm // rb)]

    def store_stage(p, b):
        slot = b % n_slots
        return pltpu.make_async_copy(stage.at[slot], nxt_hbm.at[pl.ds(p * tm + b * rb, rb), :],
                                     sems.at[2 + slot])

    def outstanding(p):
        last = range(len(row_blocks) - n_slots, len(row_blocks))
        return [store_panel(p)] + [store_stage(p, b) for b in last]

    def partial_y(cols):
        return jnp.dot(xm_ref[...], wm_ref[:, cols], preferred_element_type=F32)

    def first_y(cols):
        y = partial_y(cols)
        if has_tail:
            y = y + jnp.dot(xt_ref[...], wt_ref[:, cols], preferred_element_type=F32)
        return y

    def row_scale(ss):
        return lax.rsqrt(jnp.sum(ss, axis=-1, keepdims=True) * (1.0 / n_cols) + EPS)

    def epilogue_rows(p, b):
        rws = row_blocks[b]
        r1 = scale * row_scale(ss_scr[rws, :])
        sq = jnp.zeros((rb, V7X_LANES), F32)
        for cols in col_chunks:
            h_new = h_scr[rws, cols] + y_scr[rws, cols] * r1 * gp_ref[:, cols]
            h_scr[rws, cols] = h_new
            sq = sq + _lane_fold(h_new * h_new)
        r2 = row_scale(sq)
        if emit_residual and b >= n_slots:
            store_stage(p, b - n_slots).wait()
        for cols in col_chunks:
            nxt = h_scr[rws, cols] * r2 * gn_ref[:, cols]
            if emit_residual:
                stage[b % n_slots, :, cols] = nxt.astype(stage.dtype)
            else:
                h_scr[rws, cols] = nxt
        if emit_residual:
            store_stage(p, b).start(priority=PANEL_DMA_PRIORITY)

    has_prev = i > 0

    @pl.when(k == 0)
    def _():
        for cols in col_chunks:
            y_scr[:, cols] = first_y(cols)

    @pl.when(jnp.logical_and(k > 0, k < n_k - 1))
    def _():
        for cols in col_chunks:
            y_scr[:, cols] += partial_y(cols)

    @pl.when(k == n_k // 2)
    def _():
        @pl.when(has_prev)
        def _():
            for copy in outstanding(i - 1):
                copy.wait()
        fetch(i).start(priority=PANEL_DMA_PRIORITY)

    @pl.when(k == n_k - 1)
    def _():
        sq = jnp.zeros((tm, V7X_LANES), F32)
        for cols in col_chunks:
            y = y_scr[:, cols] + partial_y(cols)
            y_scr[:, cols] = y
            sq = sq + _lane_fold(y * y)
        ss_scr[...] = sq

    @pl.when(k == n_k - 1)
    def _():
        fetch(i).wait()
        for b in range(len(row_blocks)):
            epilogue_rows(i, b)
        store_panel(i).start(priority=PANEL_DMA_PRIORITY)

        @pl.when(i == n_i - 1)
        def _():
            for copy in outstanding(i):
                copy.wait()


def _down_residual(x_main, x_tail, w, h, gain_post, gain_next, *, scale, emit_residual,
                   tm, tk, chunk, name):
    t, k_main = x_main.shape
    k_tail = 0 if x_tail is None else x_tail.shape[1]
    n = w.shape[1]
    n_k = k_main // tk
    assert t % tm == 0 and tm % EPILOGUE_ROWS == 0 and k_main % tk == 0 and n % chunk == 0
    assert n_k >= 4 and h.shape == (t, n) and w.shape[0] == k_main + k_tail
    assert k_tail == 0 or k_main % k_tail == 0

    any_spec = pl.BlockSpec(memory_space=pl.ANY)
    once = dict(pipeline_mode=pl.Buffered(1))
    gain_spec = pl.BlockSpec((1, n), lambda i, k: (0, 0), **once)
    in_specs = [pl.BlockSpec((tm, tk), lambda i, k: (i, k))]
    operands = [x_main]
    blocks = [((tm, tk), BF16), ((tk, n), BF16)]
    if k_tail:
        in_specs.append(pl.BlockSpec((tm, k_tail), lambda i, k: (i, 0)))
        operands.append(x_tail)
        blocks.append(((tm, k_tail), BF16))
    in_specs.append(pl.BlockSpec((tk, n), lambda i, k: (k, 0)))
    operands.append(w)
    if k_tail:
        in_specs.append(pl.BlockSpec((k_tail, n), lambda i, k: (k_main // k_tail, 0), **once))
        operands.append(w)
    in_specs += [gain_spec, gain_spec, any_spec]
    operands += [gain_post, gain_next, h]

    out_shape = [jax.ShapeDtypeStruct((t, n), BF16 if emit_residual else F32)]
    scratch = [((tm, n), F32), ((tm, V7X_LANES), F32), ((tm, n), F32)]
    n_sems = 2
    if emit_residual:
        out_shape.insert(0, jax.ShapeDtypeStruct((t, n), F32))
        n_slots = min(STAGE_SLOTS, tm // EPILOGUE_ROWS)
        scratch.append(((n_slots, EPILOGUE_ROWS, n), BF16))
        n_sems = 2 + n_slots
    single = scratch + [((k_tail, n), BF16), ((2, n), F32), ((tm, chunk), F32), ((tm, chunk), F32)]
    return pl.pallas_call(
        functools.partial(_down_kernel, scale=scale, emit_residual=emit_residual,
                          has_tail=bool(k_tail), chunk=chunk),
        grid=(t // tm, n_k),
        in_specs=in_specs,
        out_specs=[any_spec] * len(out_shape),
        out_shape=out_shape,
        scratch_shapes=[pltpu.VMEM(s, d) for s, d in scratch] + [pltpu.SemaphoreType.DMA((n_sems,))],
        compiler_params=_params(("arbitrary", "arbitrary"), _vmem_limit(blocks, single)),
        name=name,
    )(*operands)


def _mixer_kernel(u_ref, v_ref, p_ref, pprev_ref, pnext_ref, ws_ref, bs_ref, wp_ref, ps_ref,
                  o_ref, *, tiles_per_seq, seq_len):
    tm = u_ref.shape[0]
    n_chunks = tm // CHUNK

    for hd in range(N_HEADS):
        cols = slice(hd * HEAD_DIM, (hd + 1) * HEAD_DIM)
        v_h = jnp.concatenate(
            [v_ref[c * CHUNK:(c + 1) * CHUNK, cols] for c in range(n_chunks)], axis=1)
        sv = jnp.dot(ws_ref[hd], v_h, preferred_element_type=F32)
        bias = bs_ref[hd]
        for c in range(n_chunks):
            rows = slice(c * CHUNK, (c + 1) * CHUNK)
            gate = sv[:, c * HEAD_DIM:(c + 1) * HEAD_DIM] + bias
            o_ref[rows, cols] = (u_ref[rows, cols].astype(F32) * gate).astype(o_ref.dtype)

    pos = pl.program_id(0) % tiles_per_seq
    prev = jnp.where(pos > 0, pprev_ref[...], 0.0)
    nxt = jnp.where(pos < tiles_per_seq - 1, pnext_ref[...], 0.0)
    row = lax.broadcasted_iota(jnp.int32, (tm, 1), 0) + pos * tm
    for g, w in enumerate(POOL_WINDOWS):
        half = w // 2
        cols = slice(g * D_POOL_GROUP, (g + 1) * D_POOL_GROUP)
        p = p_ref[:, cols]
        s = jnp.concatenate([prev[:, cols], p, nxt[:, cols]], axis=0)
        step = 1
        while step < w:
            s = s[:s.shape[0] - step] + s[step:]
            step *= 2
        win = s[POOL_HALO - half:POOL_HALO - half + tm]
        count = jnp.minimum(row + half, seq_len) - jnp.maximum(row - half, 0)
        pooled = win * (1.0 / count.astype(F32)) - p
        y_b = jnp.dot(pooled.astype(BF16), wp_ref[g], preferred_element_type=F32) * ps_ref[:, cols]
        o_ref[:, D_A + g * D_POOL_GROUP:D_A + (g + 1) * D_POOL_GROUP] = y_b.astype(o_ref.dtype)


def _mixer(u, v, p, w_spatial, b_spatial, w_pool, pool_scale, *, seq_len, tm=512):
    t = u.shape[0]
    assert seq_len % tm == 0 and tm % CHUNK == 0
    halo_blocks_per_tile = tm // POOL_HALO
    n_halo_blocks = t // POOL_HALO
    row_spec = lambda width: pl.BlockSpec((tm, width), lambda i: (i, 0))
    const3 = lambda shape: pl.BlockSpec(shape, lambda i: (0, 0, 0))
    blocks = [((tm, D_A), BF16)] * 2 + [((tm, D_B), F32), ((tm, D_MODEL), BF16),
              (w_spatial.shape, BF16), (b_spatial.shape, F32), (w_pool.shape, BF16)]
    return pl.pallas_call(
        functools.partial(_mixer_kernel, tiles_per_seq=seq_len // tm, seq_len=seq_len),
        grid=(t // tm,),
        in_specs=[
            row_spec(D_A), row_spec(D_A), row_spec(D_B),
            pl.BlockSpec((POOL_HALO, D_B),
                         lambda i: (jnp.maximum(i * halo_blocks_per_tile - 1, 0), 0)),
            pl.BlockSpec((POOL_HALO, D_B),
                         lambda i: (jnp.minimum((i + 1) * halo_blocks_per_tile, n_halo_blocks - 1), 0)),
            const3(w_spatial.shape), const3(b_spatial.shape), const3(w_pool.shape),
            pl.BlockSpec((1, D_B), lambda i: (0, 0)),
        ],
        out_specs=row_spec(D_MODEL),
        out_shape=jax.ShapeDtypeStruct((t, D_MODEL), BF16),
        compiler_params=_params(("parallel",),
                                _vmem_limit(blocks, [((tm + 2 * POOL_HALO, D_B), F32)] * 3)),
        name="token_mixer",
    )(u, v, p, p, p, w_spatial, b_spatial, w_pool, pool_scale)


_FFN_UP = functools.partial(_ffn_up, tm=1024, tn=512)
_DOWN = functools.partial(_down_residual, tm=1024, tk=512, chunk=512)


def _trunk_front(x, w, side, side_rows):
    b, s, d = x.shape
    h0 = x.reshape(b * s, d)
    xn = _rmsnorm_rows(h0, w["ffn1_pre"])
    act, act_tail = _FFN_UP(xn, w["ffn1_gate"], w["ffn1_up"])
    h1, a = _DOWN(act, act_tail, w["ffn1_down"], h0, w["ffn1_post"], w["mix_pre"],
                  scale=FFN_RESIDUAL, emit_residual=True, name="ffn1_down")

    u, v, p, side_bf16 = _w_in(a, w["w_in"], w["v_norm"], side, tm=1024, tn=512, side_rows=side_rows)
    y_cat = _mixer(u, v, p, w["w_spatial"], w["b_spatial"], w["w_pool"], w["pool_scale"], seq_len=s)
    h2, xn = _DOWN(y_cat, None, w["w_out"], h1, w["mix_post"], w["ffn2_pre"],
                   scale=1.0, emit_residual=True, name="w_out")
    return h2, xn, side_bf16


def _trunk_back(h2, xn, w, out_shape):
    act, act_tail = _FFN_UP(xn, w["ffn2_gate"], w["ffn2_up"])
    (out,) = _DOWN(act, act_tail, w["ffn2_down"], h2, w["ffn2_post"], w["final_norm"],
                   scale=FFN_RESIDUAL, emit_residual=False, name="ffn2_down")
    return out.reshape(out_shape)


def kernel(x_prompt, x_sample, ffn1_pre, ffn1_post, ffn1_gate, ffn1_up, ffn1_down, mix_pre, mix_post, w_in, w_spatial, b_spatial, v_norm, w_pool, pool_scale, w_out, ffn2_pre, ffn2_post, ffn2_gate, ffn2_up, ffn2_down, final_norm):
    assert ffn1_pre.shape[0] == 1, "single-layer trunk"
    row = lambda g: g[0].reshape(1, -1).astype(F32)
    mxu = lambda m: m[0].astype(BF16)
    w = {
        "ffn1_pre": row(ffn1_pre), "ffn1_post": row(ffn1_post),
        "ffn1_gate": mxu(ffn1_gate), "ffn1_up": mxu(ffn1_up), "ffn1_down": mxu(ffn1_down),
        "mix_pre": row(mix_pre), "mix_post": row(mix_post),
        "w_in": mxu(w_in),
        "w_spatial": mxu(w_spatial),
        "b_spatial": jnp.broadcast_to(b_spatial[0][:, :, None], (N_HEADS, CHUNK, HEAD_DIM)).astype(F32),
        "v_norm": row(v_norm), "w_pool": mxu(w_pool), "pool_scale": row(pool_scale),
        "w_out": mxu(w_out),
        "ffn2_pre": row(ffn2_pre), "ffn2_post": row(ffn2_post),
        "final_norm": row(final_norm),
    }
    h2_p, xn_p, (w["ffn2_gate"], w["ffn2_up"]) = _trunk_front(
        x_prompt, w, (ffn2_gate[0].astype(F32), ffn2_up[0].astype(F32)), FFN_UP_CAST_ROWS)
    h2_s, xn_s, (w["ffn2_down"],) = _trunk_front(
        x_sample, w, (ffn2_down[0].astype(F32),), FFN_DOWN_CAST_ROWS)
    return (_trunk_back(h2_p, xn_p, w, x_prompt.shape), _trunk_back(h2_s, xn_s, w, x_sample.shape))
```

```python
import functools
import math

import jax
import jax.numpy as jnp
from jax import lax
from jax.experimental import pallas as pl
from jax.experimental.pallas import tpu as pltpu

F32 = jnp.float32
BF16 = jnp.bfloat16

D_MODEL = 4096
D_A = D_MODEL // 2
HEAD_DIM = 128
N_HEADS = D_A // HEAD_DIM
CHUNK = 128
D_B = D_MODEL - D_A
POOL_WINDOWS = (2, 4, 8, 16)
D_POOL_GROUP = D_B // len(POOL_WINDOWS)
POOL_HALO = max(POOL_WINDOWS) // 2
EPS = 1e-6
FFN_RESIDUAL = 0.5

V7X_LANES = 128
V7X_VMEM_BYTES = 64 * 1024 * 1024
COMPILER_SCRATCH_BYTES = 4 * 1024 * 1024
PANEL_DMA_PRIORITY = 1


def _nbytes(shape, dtype):
    return math.prod(shape) * jnp.dtype(dtype).itemsize


def _vmem_limit(pipelined_blocks, temporaries=()):
    need = 2 * sum(_nbytes(s, d) for s, d in pipelined_blocks)
    need += sum(_nbytes(s, d) for s, d in temporaries)
    need += COMPILER_SCRATCH_BYTES
    assert need <= V7X_VMEM_BYTES, need
    return need


def _params(semantics, vmem_bytes):
    return pltpu.CompilerParams(dimension_semantics=semantics, vmem_limit_bytes=vmem_bytes)


def _rms_scale(x):
    return lax.rsqrt(jnp.mean(x * x, axis=-1, keepdims=True) + EPS)


def _gelu(x):
    return 0.5 * x * (1.0 + lax.erf(x * math.sqrt(0.5)))


def _lane_fold(x):
    acc = x[:, :V7X_LANES]
    for c in range(1, x.shape[1] // V7X_LANES):
        acc = acc + x[:, c * V7X_LANES:(c + 1) * V7X_LANES]
    return acc


def _split_tail(n, tile):
    n_tail = n % tile
    n_main = n - n_tail
    assert n_tail % V7X_LANES == 0 and (n_tail == 0 or n_main % n_tail == 0)
    return n_main, n_tail


def _norm_kernel(x_ref, g_ref, o_ref):
    x = x_ref[...]
    o_ref[...] = (x * _rms_scale(x) * g_ref[...]).astype(o_ref.dtype)


def _rmsnorm_rows(x, gain, *, rows=256):
    t, d = x.shape
    blocks = [((rows, d), F32), ((rows, d), BF16)]
    return pl.pallas_call(
        _norm_kernel,
        grid=(t // rows,),
        in_specs=[pl.BlockSpec((rows, d), lambda i: (i, 0)),
                  pl.BlockSpec((1, d), lambda i: (0, 0))],
        out_specs=pl.BlockSpec((rows, d), lambda i: (i, 0)),
        out_shape=jax.ShapeDtypeStruct((t, d), BF16),
        compiler_params=_params(("parallel",), _vmem_limit(blocks, [((rows, d), F32)] * 2)),
        name="rmsnorm_rows",
    )(x, gain)


FFN1_DOWN_CAST_ROWS = 64
W_IN_CAST_ROWS = 32
FFN_UP_CAST_ROWS = 64
SQUARE_CAST_ROWS = 256


def _side_cast_step(step, srcs, dsts, f32_stage, bf16_stage, sems):
    rows = f32_stage.shape[1]
    starts = [0]
    for src in srcs:
        starts.append(starts[-1] + src.shape[0] // rows)
    n_blocks = starts[-1]

    def per_array(blk, fn):
        for a, (src, dst) in enumerate(zip(srcs, dsts)):
            @pl.when(jnp.logical_and(blk >= starts[a], blk < starts[a + 1]))
            def _():
                fn(src, dst, pl.ds((blk - starts[a]) * rows, rows))

    def load(src, block_rows, blk):
        return pltpu.make_async_copy(src.at[block_rows, :], f32_stage.at[blk % 2], sems.at[blk % 2])

    def store(dst, block_rows, blk):
        return pltpu.make_async_copy(bf16_stage.at[blk % 2], dst.at[block_rows, :], sems.at[2 + blk % 2])

    first_rows = pl.ds(0, rows)

    @pl.when(step == 0)
    def _():
        load(srcs[0], first_rows, step).start(priority=PANEL_DMA_PRIORITY)

    @pl.when(step + 1 < n_blocks)
    def _():
        per_array(step + 1, lambda src, dst, r: load(src, r, step + 1).start(priority=PANEL_DMA_PRIORITY))

    @pl.when(step < n_blocks)
    def _():
        load(srcs[0], first_rows, step).wait()

        @pl.when(step >= 2)
        def _():
            store(dsts[0], first_rows, step - 2).wait()
        bf16_stage[step % 2] = f32_stage[step % 2].astype(bf16_stage.dtype)
        per_array(step, lambda src, dst, r: store(dst, r, step).start(priority=PANEL_DMA_PRIORITY))

    @pl.when(step == n_blocks - 1)
    def _():
        if n_blocks > 1:
            store(dsts[0], first_rows, step - 1).wait()
        store(dsts[0], first_rows, step).wait()


def _side_cast_plan(side, side_rows, n_steps):
    if not side:
        return [], [], []
    cols = side[0].shape[1]
    assert all(s.shape[1] == cols and s.dtype == F32 and s.shape[0] % side_rows == 0 for s in side)
    assert sum(s.shape[0] // side_rows for s in side) <= n_steps
    any_specs = [pl.BlockSpec(memory_space=pl.ANY)] * len(side)
    out_shape = [jax.ShapeDtypeStruct(s.shape, BF16) for s in side]
    scratch = [((2, side_rows, cols), F32), ((2, side_rows, cols), BF16)]
    return any_specs, out_shape, scratch


def _scratch_shapes(side_scratch):
    if not side_scratch:
        return []
    return [pltpu.VMEM(s, d) for s, d in side_scratch] + [pltpu.SemaphoreType.DMA((4,))]


def _swiglu_tile(x, wg_ref, wu_ref):
    g = jnp.dot(x, wg_ref[...], preferred_element_type=F32)
    u = jnp.dot(x, wu_ref[...], preferred_element_type=F32)
    return (g * jax.nn.sigmoid(g) * u).astype(BF16)


def _ffn_up_kernel(x_ref, wg_ref, wu_ref, *rest, has_tail, n_side):
    rest = list(rest)
    wgt_ref, wut_ref = (rest.pop(0), rest.pop(0)) if has_tail else (None, None)
    side_srcs = [rest.pop(0) for _ in range(n_side)]
    o_ref = rest.pop(0)
    ot_ref = rest.pop(0) if has_tail else None
    side_dsts = [rest.pop(0) for _ in range(n_side)]
    if n_side:
        step = pl.program_id(0) * pl.num_programs(1) + pl.program_id(1)
        _side_cast_step(step, side_srcs, side_dsts, *rest)
    o_ref[...] = _swiglu_tile(x_ref[...], wg_ref, wu_ref)
    if has_tail:
        @pl.when(pl.program_id(1) == 0)
        def _():
            ot_ref[...] = _swiglu_tile(x_ref[...], wgt_ref, wut_ref)


def _ffn_up(x, w_gate, w_up, side=(), *, tm, tn, side_rows=None):
    t, k = x.shape
    n_main, n_tail = _split_tail(w_gate.shape[1], tn)
    assert t % tm == 0
    grid = (t // tm, n_main // tn)
    side_specs, side_out, side_scratch = _side_cast_plan(side, side_rows, grid[0] * grid[1])
    w_spec = pl.BlockSpec((k, tn), lambda i, j: (0, j))
    in_specs = [pl.BlockSpec((tm, k), lambda i, j: (i, 0)), w_spec, w_spec]
    out_specs = [pl.BlockSpec((tm, tn), lambda i, j: (i, j))]
    out_shape = [jax.ShapeDtypeStruct((t, n_main), BF16)]
    operands = [x, w_gate, w_up]
    blocks = [((tm, k), x.dtype), ((k, tn), BF16), ((k, tn), BF16), ((tm, tn), BF16)]
    if n_tail:
        tail_spec = pl.BlockSpec((k, n_tail), lambda i, j: (0, n_main // n_tail),
                                 pipeline_mode=pl.Buffered(1))
        in_specs += [tail_spec, tail_spec]
        out_specs.append(pl.BlockSpec((tm, n_tail), lambda i, j: (i, 0)))
        out_shape.append(jax.ShapeDtypeStruct((t, n_tail), BF16))
        operands += [w_gate, w_up]
        blocks += [((k, n_tail), BF16), ((tm, n_tail), BF16)]
    outs = pl.pallas_call(
        functools.partial(_ffn_up_kernel, has_tail=bool(n_tail), n_side=len(side)),
        grid=grid,
        in_specs=in_specs + side_specs,
        out_specs=out_specs + side_specs,
        out_shape=out_shape + side_out,
        scratch_shapes=_scratch_shapes(side_scratch),
        compiler_params=_params(("arbitrary", "arbitrary"),
                                _vmem_limit(blocks, side_scratch + [((tm, tn), F32)] * 4)),
        name="ffn_up",
    )(*operands, *side)
    n_out = 2 if n_tail else 1
    return outs[0], (outs[1] if n_tail else None), list(outs[n_out:])


def _w_in_kernel(x_ref, w_ref, vg_ref, *rest, n_u, n_v, n_side):
    side_srcs, rest = rest[:n_side], rest[n_side:]
    (u_ref, v_ref, p_ref), rest = rest[:3], rest[3:]
    side_dsts, scratch = rest[:n_side], rest[n_side:]
    j = pl.program_id(1)
    if n_side:
        _side_cast_step(pl.program_id(0) * pl.num_programs(1) + j, side_srcs, side_dsts, *scratch)

    matmul = lambda: jnp.dot(x_ref[...], w_ref[...], preferred_element_type=F32)

    @pl.when(j < n_u)
    def _():
        u_ref[...] = _gelu(matmul()).astype(u_ref.dtype)

    @pl.when(jnp.logical_and(j >= n_u, j < n_u + n_v))
    def _():
        g = _gelu(matmul())
        for hd in range(v_ref.shape[1] // HEAD_DIM):
            cols = slice(hd * HEAD_DIM, (hd + 1) * HEAD_DIM)
            gh = g[:, cols]
            v_ref[:, cols] = (gh * _rms_scale(gh) * vg_ref[:, cols]).astype(v_ref.dtype)

    @pl.when(j >= n_u + n_v)
    def _():
        p_ref[...] = matmul()


def _w_in(x, w, v_gain, side=(), *, tm, tn, side_rows=None):
    t, k = x.shape
    assert t % tm == 0 and D_A % tn == 0 and D_B % tn == 0 and w.shape[1] == 2 * D_A + D_B
    n_u = n_v = D_A // tn
    n_p = D_B // tn
    grid = (t // tm, n_u + n_v + n_p)
    clamp = lambda j, lo, cnt: jnp.clip(j - lo, 0, cnt - 1)
    side_specs, side_out, side_scratch = _side_cast_plan(side, side_rows, grid[0] * grid[1])
    blocks = [((tm, k), x.dtype), ((k, tn), BF16), ((tm, tn), BF16), ((tm, tn), BF16), ((tm, tn), F32)]
    outs = pl.pallas_call(
        functools.partial(_w_in_kernel, n_u=n_u, n_v=n_v, n_side=len(side)),
        grid=grid,
        in_specs=[pl.BlockSpec((tm, k), lambda i, j: (i, 0)),
                  pl.BlockSpec((k, tn), lambda i, j: (0, j)),
                  pl.BlockSpec((1, tn), lambda i, j: (0, clamp(j, n_u, n_v)))] + side_specs,
        out_specs=[pl.BlockSpec((tm, tn), lambda i, j: (i, clamp(j, 0, n_u))),
                   pl.BlockSpec((tm, tn), lambda i, j: (i, clamp(j, n_u, n_v))),
                   pl.BlockSpec((tm, tn), lambda i, j: (i, clamp(j, n_u + n_v, n_p)))] + side_specs,
        out_shape=[jax.ShapeDtypeStruct((t, D_A), BF16), jax.ShapeDtypeStruct((t, D_A), BF16),
                   jax.ShapeDtypeStruct((t, D_B), F32)] + side_out,
        scratch_shapes=_scratch_shapes(side_scratch),
        compiler_params=_params(("arbitrary", "arbitrary"),
                                _vmem_limit(blocks, side_scratch + [((tm, tn), F32)] * 3)),
        name="w_in",
    )(x, w, v_gain, *side)
    return outs[0], outs[1], outs[2], list(outs[3:])


EPILOGUE_ROWS = 128
STAGE_SLOTS = 4


def _down_kernel(*refs, scale, emit_residual, has_tail, chunk):
    refs = list(refs)
    xm_ref = refs.pop(0)
    xt_ref = refs.pop(0) if has_tail else None
    wm_ref = refs.pop(0)
    wt_ref = refs.pop(0) if has_tail else None
    gp_ref, gn_ref, h_hbm = refs.pop(0), refs.pop(0), refs.pop(0)
    hout_hbm = refs.pop(0) if emit_residual else None
    nxt_hbm = refs.pop(0)
    y_scr, ss_scr, h_scr = refs.pop(0), refs.pop(0), refs.pop(0)
    stage = refs.pop(0) if emit_residual else None
    (sems,) = refs

    i, k = pl.program_id(0), pl.program_id(1)
    n_i, n_k = pl.num_programs(0), pl.num_programs(1)
    tm, n_cols = y_scr.shape
    col_chunks = [slice(c * chunk, (c + 1) * chunk) for c in range(n_cols // chunk)]
    n_slots = stage.shape[0] if emit_residual else 0
    rows = lambda p: pl.ds(p * tm, tm)

    def fetch(p):
        return pltpu.make_async_copy(h_hbm.at[rows(p), :], h_scr, sems.at[0])

    def store_panel(p):
        dst = hout_hbm if emit_residual else nxt_hbm
        return pltpu.make_async_copy(h_scr, dst.at[rows(p), :], sems.at[1])

    rb = stage.shape[1] if emit_residual else EPILOGUE_ROWS
    row_blocks = [slice(b * rb, (b + 1) * rb) for b in range(tm // rb)]

    def store_stage(p, b):
        slot = b % n_slots
        return pltpu.make_async_copy(stage.at[slot], nxt_hbm.at[pl.ds(p * tm + b * rb, rb), :],
                                     sems.at[2 + slot])

    def outstanding(p):
        last = range(len(row_blocks) - n_slots, len(row_blocks))
        return [store_panel(p)] + [store_stage(p, b) for b in last]

    def partial_y(cols):
        return jnp.dot(xm_ref[...], wm_ref[:, cols], preferred_element_type=F32)

    def first_y(cols):
        y = partial_y(cols)
        if has_tail:
            y = y + jnp.dot(xt_ref[...], wt_ref[:, cols], preferred_element_type=F32)
        return y

    def row_scale(ss):
        return lax.rsqrt(jnp.sum(ss, axis=-1, keepdims=True) * (1.0 / n_cols) + EPS)

    def epilogue_rows(p, b):
        rws = row_blocks[b]
        r1 = scale * row_scale(ss_scr[rws, :])
        sq = jnp.zeros((rb, V7X_LANES), F32)
        for cols in col_chunks:
            h_new = h_scr[rws, cols] + y_scr[rws, cols] * r1 * gp_ref[:, cols]
            h_scr[rws, cols] = h_new
            sq = sq + _lane_fold(h_new * h_new)
        r2 = row_scale(sq)
        if emit_residual and b >= n_slots:
            store_stage(p, b - n_slots).wait()
        for cols in col_chunks:
            nxt = h_scr[rws, cols] * r2 * gn_ref[:, cols]
            if emit_residual:
                stage[b % n_slots, :, cols] = nxt.astype(stage.dtype)
            else:
                h_scr[rws, cols] = nxt
        if emit_residual:
            store_stage(p, b).start(priority=PANEL_DMA_PRIORITY)

    has_prev = i > 0

    @pl.when(k == 0)
    def _():
        for cols in col_chunks:
            y_scr[:, cols] = first_y(cols)

    @pl.when(jnp.logical_and(k > 0, k < n_k - 1))
    def _():
        for cols in col_chunks:
            y_scr[:, cols] += partial_y(cols)

    @pl.when(k == n_k // 2)
    def _():
        @pl.when(has_prev)
        def _():
            for copy in outstanding(i - 1):
                copy.wait()
        fetch(i).start(priority=PANEL_DMA_PRIORITY)

    @pl.when(k == n_k - 1)
    def _():
        sq = jnp.zeros((tm, V7X_LANES), F32)
        for cols in col_chunks:
            y = y_scr[:, cols] + partial_y(cols)
            y_scr[:, cols] = y
            sq = sq + _lane_fold(y * y)
        ss_scr[...] = sq

    @pl.when(k == n_k - 1)
    def _():
        fetch(i).wait()
        for b in range(len(row_blocks)):
            epilogue_rows(i, b)
        store_panel(i).start(priority=PANEL_DMA_PRIORITY)

        @pl.when(i == n_i - 1)
        def _():
            for copy in outstanding(i):
                copy.wait()


def _down_residual(x_main, x_tail, w, h, gain_post, gain_next, *, scale, emit_residual,
                   tm, tk, chunk, name):
    t, k_main = x_main.shape
    k_tail = 0 if x_tail is None else x_tail.shape[1]
    n = w.shape[1]
    n_k = k_main // tk
    assert t % tm == 0 and tm % EPILOGUE_ROWS == 0 and k_main % tk == 0 and n % chunk == 0
    assert n_k >= 4 and h.shape == (t, n) and w.shape[0] == k_main + k_tail
    assert k_tail == 0 or k_main % k_tail == 0

    any_spec = pl.BlockSpec(memory_space=pl.ANY)
    once = dict(pipeline_mode=pl.Buffered(1))
    gain_spec = pl.BlockSpec((1, n), lambda i, k: (0, 0), **once)
    in_specs = [pl.BlockSpec((tm, tk), lambda i, k: (i, k))]
    operands = [x_main]
    blocks = [((tm, tk), BF16), ((tk, n), BF16)]
    if k_tail:
        in_specs.append(pl.BlockSpec((tm, k_tail), lambda i, k: (i, 0)))
        operands.append(x_tail)
        blocks.append(((tm, k_tail), BF16))
    in_specs.append(pl.BlockSpec((tk, n), lambda i, k: (k, 0)))
    operands.append(w)
    if k_tail:
        in_specs.append(pl.BlockSpec((k_tail, n), lambda i, k: (k_main // k_tail, 0), **once))
        operands.append(w)
    in_specs += [gain_spec, gain_spec, any_spec]
    operands += [gain_post, gain_next, h]

    out_shape = [jax.ShapeDtypeStruct((t, n), BF16 if emit_residual else F32)]
    scratch = [((tm, n), F32), ((tm, V7X_LANES), F32), ((tm, n), F32)]
    n_sems = 2
    if emit_residual:
        out_shape.insert(0, jax.ShapeDtypeStruct((t, n), F32))
        n_slots = min(STAGE_SLOTS, tm // EPILOGUE_ROWS)
        scratch.append(((n_slots, EPILOGUE_ROWS, n), BF16))
        n_sems = 2 + n_slots
    single = scratch + [((k_tail, n), BF16), ((2, n), F32), ((tm, chunk), F32), ((tm, chunk), F32)]
    return pl.pallas_call(
        functools.partial(_down_kernel, scale=scale, emit_residual=emit_residual,
                          has_tail=bool(k_tail), chunk=chunk),
        grid=(t // tm, n_k),
        in_specs=in_specs,
        out_specs=[any_spec] * len(out_shape),
        out_shape=out_shape,
        scratch_shapes=[pltpu.VMEM(s, d) for s, d in scratch] + [pltpu.SemaphoreType.DMA((n_sems,))],
        compiler_params=_params(("arbitrary", "arbitrary"), _vmem_limit(blocks, single)),
        name=name,
    )(*operands)


def _mixer_kernel(u_ref, v_ref, p_ref, pprev_ref, pnext_ref, ws_ref, bs_ref, wp_ref, ps_ref,
                  o_ref, *, tiles_per_seq, seq_len):
    tm = u_ref.shape[0]
    n_chunks = tm // CHUNK

    for hd in range(N_HEADS):
        cols = slice(hd * HEAD_DIM, (hd + 1) * HEAD_DIM)
        v_h = jnp.concatenate(
            [v_ref[c * CHUNK:(c + 1) * CHUNK, cols] for c in range(n_chunks)], axis=1)
        sv = jnp.dot(ws_ref[hd], v_h, preferred_element_type=F32)
        bias = bs_ref[hd]
        for c in range(n_chunks):
            rows = slice(c * CHUNK, (c + 1) * CHUNK)
            gate = sv[:, c * HEAD_DIM:(c + 1) * HEAD_DIM] + bias
            o_ref[rows, cols] = (u_ref[rows, cols].astype(F32) * gate).astype(o_ref.dtype)

    pos = pl.program_id(0) % tiles_per_seq
    prev = jnp.where(pos > 0, pprev_ref[...], 0.0)
    nxt = jnp.where(pos < tiles_per_seq - 1, pnext_ref[...], 0.0)
    row = lax.broadcasted_iota(jnp.int32, (tm, 1), 0) + pos * tm
    for g, w in enumerate(POOL_WINDOWS):
        half = w // 2
        cols = slice(g * D_POOL_GROUP, (g + 1) * D_POOL_GROUP)
        p = p_ref[:, cols]
        s = jnp.concatenate([prev[:, cols], p, nxt[:, cols]], axis=0)
        step = 1
        while step < w:
            s = s[:s.shape[0] - step] + s[step:]
            step *= 2
        win = s[POOL_HALO - half:POOL_HALO - half + tm]
        count = jnp.minimum(row + half, seq_len) - jnp.maximum(row - half, 0)
        pooled = win * (1.0 / count.astype(F32)) - p
        y_b = jnp.dot(pooled.astype(BF16), wp_ref[g], preferred_element_type=F32) * ps_ref[:, cols]
        o_ref[:, D_A + g * D_POOL_GROUP:D_A + (g + 1) * D_POOL_GROUP] = y_b.astype(o_ref.dtype)


def _mixer(u, v, p, w_spatial, b_spatial, w_pool, pool_scale, *, seq_len, tm=512):
    t = u.shape[0]
    assert seq_len % tm == 0 and tm % CHUNK == 0
    halo_blocks_per_tile = tm // POOL_HALO
    n_halo_blocks = t // POOL_HALO
    row_spec = lambda width: pl.BlockSpec((tm, width), lambda i: (i, 0))
    const3 = lambda shape: pl.BlockSpec(shape, lambda i: (0, 0, 0))
    blocks = [((tm, D_A), BF16)] * 2 + [((tm, D_B), F32), ((tm, D_MODEL), BF16),
              (w_spatial.shape, BF16), (b_spatial.shape, F32), (w_pool.shape, BF16)]
    return pl.pallas_call(
        functools.partial(_mixer_kernel, tiles_per_seq=seq_len // tm, seq_len=seq_len),
        grid=(t // tm,),
        in_specs=[
            row_spec(D_A), row_spec(D_A), row_spec(D_B),
            pl.BlockSpec((POOL_HALO, D_B),
                         lambda i: (jnp.maximum(i * halo_blocks_per_tile - 1, 0), 0)),
            pl.BlockSpec((POOL_HALO, D_B),
                         lambda i: (jnp.minimum((i + 1) * halo_blocks_per_tile, n_halo_blocks - 1), 0)),
            const3(w_spatial.shape), const3(b_spatial.shape), const3(w_pool.shape),
            pl.BlockSpec((1, D_B), lambda i: (0, 0)),
        ],
        out_specs=row_spec(D_MODEL),
        out_shape=jax.ShapeDtypeStruct((t, D_MODEL), BF16),
        compiler_params=_params(("parallel",),
                                _vmem_limit(blocks, [((tm + 2 * POOL_HALO, D_B), F32)] * 3)),
        name="token_mixer",
    )(u, v, p, p, p, w_spatial, b_spatial, w_pool, pool_scale)


_FFN_UP = functools.partial(_ffn_up, tm=1024, tn=512)
_DOWN = functools.partial(_down_residual, tm=1024, tk=512, chunk=512)
_W_IN = functools.partial(_w_in, tm=1024, tn=512)


def _as_rows(x):
    b, s, d = x.shape
    return x.reshape(b * s, d)


def _mix_out(u, v, p, h1, w, seq_len):
    y_cat = _mixer(u, v, p, w["w_spatial"], w["b_spatial"], w["w_pool"], w["pool_scale"], seq_len=seq_len)
    return _DOWN(y_cat, None, w["w_out"], h1, w["mix_post"], w["ffn2_pre"],
                 scale=1.0, emit_residual=True, name="w_out")


def _ffn2(h2, xn, w, out_shape):
    act, act_tail, _ = _FFN_UP(xn, w["ffn2_gate"], w["ffn2_up"])
    (out,) = _DOWN(act, act_tail, w["ffn2_down"], h2, w["ffn2_post"], w["final_norm"],
                   scale=FFN_RESIDUAL, emit_residual=False, name="ffn2_down")
    return out.reshape(out_shape)


def kernel(x_prompt, x_sample, ffn1_pre, ffn1_post, ffn1_gate, ffn1_up, ffn1_down, mix_pre, mix_post, w_in, w_spatial, b_spatial, v_norm, w_pool, pool_scale, w_out, ffn2_pre, ffn2_post, ffn2_gate, ffn2_up, ffn2_down, final_norm):
    assert ffn1_pre.shape[0] == 1, "single-layer trunk"
    row = lambda g: g[0].reshape(1, -1).astype(F32)
    mxu = lambda m: m[0].astype(BF16)
    raw = lambda m: m[0].astype(F32)
    w = {
        "ffn1_pre": row(ffn1_pre), "ffn1_post": row(ffn1_post),
        "ffn1_gate": mxu(ffn1_gate), "ffn1_up": mxu(ffn1_up),
        "mix_pre": row(mix_pre), "mix_post": row(mix_post),
        "w_spatial": mxu(w_spatial),
        "b_spatial": jnp.broadcast_to(b_spatial[0][:, :, None], (N_HEADS, CHUNK, HEAD_DIM)).astype(F32),
        "v_norm": row(v_norm), "w_pool": mxu(w_pool), "pool_scale": row(pool_scale),
        "ffn2_pre": row(ffn2_pre), "ffn2_post": row(ffn2_post),
        "final_norm": row(final_norm),
    }
    h0_p, h0_s = _as_rows(x_prompt), _as_rows(x_sample)
    act_p, tail_p, (w["ffn1_down"],) = _FFN_UP(
        _rmsnorm_rows(h0_p, w["ffn1_pre"]), w["ffn1_gate"], w["ffn1_up"], (raw(ffn1_down),),
        side_rows=FFN1_DOWN_CAST_ROWS)
    act_s, tail_s, (w["w_in"],) = _FFN_UP(
        _rmsnorm_rows(h0_s, w["ffn1_pre"]), w["ffn1_gate"], w["ffn1_up"], (raw(w_in),),
        side_rows=W_IN_CAST_ROWS)
    ffn1_down_args = dict(scale=FFN_RESIDUAL, emit_residual=True, name="ffn1_down")
    h1_p, a_p = _DOWN(act_p, tail_p, w["ffn1_down"], h0_p, w["ffn1_post"], w["mix_pre"], **ffn1_down_args)
    h1_s, a_s = _DOWN(act_s, tail_s, w["ffn1_down"], h0_s, w["ffn1_post"], w["mix_pre"], **ffn1_down_args)

    u_s, v_s, p_s, (w["ffn2_down"], w["w_out"]) = _W_IN(
        a_s, w["w_in"], w["v_norm"], (raw(ffn2_down), raw(w_out)), side_rows=SQUARE_CAST_ROWS)
    u_p, v_p, p_p, (w["ffn2_gate"], w["ffn2_up"]) = _W_IN(
        a_p, w["w_in"], w["v_norm"], (raw(ffn2_gate), raw(ffn2_up)), side_rows=FFN_UP_CAST_ROWS)
    h2_p, xn_p = _mix_out(u_p, v_p, p_p, h1_p, w, x_prompt.shape[1])
    h2_s, xn_s = _mix_out(u_s, v_s, p_s, h1_s, w, x_sample.shape[1])
    return _ffn2(h2_p, xn_p, w, x_prompt.shape), _ffn2(h2_s, xn_s, w, x_sample.shape)
```

```python
import functools
import math

import jax
import jax.numpy as jnp
from jax import lax
from jax.experimental import pallas as pl
from jax.experimental.pallas import tpu as pltpu

F32 = jnp.float32
BF16 = jnp.bfloat16

D_MODEL = 4096
D_A = D_MODEL // 2
HEAD_DIM = 128
N_HEADS = D_A // HEAD_DIM
CHUNK = 128
D_B = D_MODEL - D_A
POOL_WINDOWS = (2, 4, 8, 16)
D_POOL_GROUP = D_B // len(POOL_WINDOWS)
POOL_HALO = max(POOL_WINDOWS) // 2
EPS = 1e-6
FFN_RESIDUAL = 0.5

V7X_LANES = 128
V7X_VMEM_BYTES = 64 * 1024 * 1024
COMPILER_SCRATCH_BYTES = 4 * 1024 * 1024
PANEL_DMA_PRIORITY = 1


def _nbytes(shape, dtype):
    return math.prod(shape) * jnp.dtype(dtype).itemsize


def _vmem_limit(pipelined_blocks, temporaries=()):
    need = 2 * sum(_nbytes(s, d) for s, d in pipelined_blocks)
    need += sum(_nbytes(s, d) for s, d in temporaries)
    need += COMPILER_SCRATCH_BYTES
    assert need <= V7X_VMEM_BYTES, need
    return need


def _params(semantics, vmem_bytes):
    return pltpu.CompilerParams(dimension_semantics=semantics, vmem_limit_bytes=vmem_bytes)


def _rms_scale(x):
    return lax.rsqrt(jnp.mean(x * x, axis=-1, keepdims=True) + EPS)


def _gelu(x):
    return 0.5 * x * (1.0 + lax.erf(x * math.sqrt(0.5)))


def _lane_fold(x):
    acc = x[:, :V7X_LANES]
    for c in range(1, x.shape[1] // V7X_LANES):
        acc = acc + x[:, c * V7X_LANES:(c + 1) * V7X_LANES]
    return acc


def _split_tail(n, tile):
    n_tail = n % tile
    n_main = n - n_tail
    assert n_tail % V7X_LANES == 0 and (n_tail == 0 or n_main % n_tail == 0)
    return n_main, n_tail


def _norm_kernel(x_ref, g_ref, o_ref):
    x = x_ref[...]
    o_ref[...] = (x * _rms_scale(x) * g_ref[...]).astype(o_ref.dtype)


def _rmsnorm_rows(x, gain, *, rows=512):
    t, d = x.shape
    blocks = [((rows, d), F32), ((rows, d), BF16)]
    return pl.pallas_call(
        _norm_kernel,
        grid=(t // rows,),
        in_specs=[pl.BlockSpec((rows, d), lambda i: (i, 0)),
                  pl.BlockSpec((1, d), lambda i: (0, 0))],
        out_specs=pl.BlockSpec((rows, d), lambda i: (i, 0)),
        out_shape=jax.ShapeDtypeStruct((t, d), BF16),
        compiler_params=_params(("parallel",), _vmem_limit(blocks, [((rows, d), F32)] * 2)),
        name="rmsnorm_rows",
    )(x, gain)


FFN1_DOWN_CAST_ROWS = 64
W_IN_CAST_ROWS = 32
FFN_UP_CAST_ROWS = 64
SQUARE_CAST_ROWS = 256


def _side_cast_step(step, srcs, dsts, f32_stage, bf16_stage, sems):
    rows = f32_stage.shape[1]
    starts = [0]
    for src in srcs:
        starts.append(starts[-1] + src.shape[0] // rows)
    n_blocks = starts[-1]

    def per_array(blk, fn):
        for a, (src, dst) in enumerate(zip(srcs, dsts)):
            @pl.when(jnp.logical_and(blk >= starts[a], blk < starts[a + 1]))
            def _():
                fn(src, dst, pl.ds((blk - starts[a]) * rows, rows))

    def load(src, block_rows, blk):
        return pltpu.make_async_copy(src.at[block_rows, :], f32_stage.at[blk % 2], sems.at[blk % 2])

    def store(dst, block_rows, blk):
        return pltpu.make_async_copy(bf16_stage.at[blk % 2], dst.at[block_rows, :], sems.at[2 + blk % 2])

    first_rows = pl.ds(0, rows)

    @pl.when(step == 0)
    def _():
        load(srcs[0], first_rows, step).start(priority=PANEL_DMA_PRIORITY)

    @pl.when(step + 1 < n_blocks)
    def _():
        per_array(step + 1, lambda src, dst, r: load(src, r, step + 1).start(priority=PANEL_DMA_PRIORITY))

    @pl.when(step < n_blocks)
    def _():
        load(srcs[0], first_rows, step).wait()

        @pl.when(step >= 2)
        def _():
            store(dsts[0], first_rows, step - 2).wait()
        bf16_stage[step % 2] = f32_stage[step % 2].astype(bf16_stage.dtype)
        per_array(step, lambda src, dst, r: store(dst, r, step).start(priority=PANEL_DMA_PRIORITY))

    @pl.when(step == n_blocks - 1)
    def _():
        if n_blocks > 1:
            store(dsts[0], first_rows, step - 1).wait()
        store(dsts[0], first_rows, step).wait()


def _side_cast_plan(side, side_rows, n_steps):
    if not side:
        return [], [], []
    cols = side[0].shape[1]
    assert all(s.shape[1] == cols and s.dtype == F32 and s.shape[0] % side_rows == 0 for s in side)
    assert sum(s.shape[0] // side_rows for s in side) <= n_steps
    any_specs = [pl.BlockSpec(memory_space=pl.ANY)] * len(side)
    out_shape = [jax.ShapeDtypeStruct(s.shape, BF16) for s in side]
    scratch = [((2, side_rows, cols), F32), ((2, side_rows, cols), BF16)]
    return any_specs, out_shape, scratch


def _scratch_shapes(side_scratch):
    if not side_scratch:
        return []
    return [pltpu.VMEM(s, d) for s, d in side_scratch] + [pltpu.SemaphoreType.DMA((4,))]


def _swiglu_tile(x, wg_ref, wu_ref):
    g = jnp.dot(x, wg_ref[...], preferred_element_type=F32)
    u = jnp.dot(x, wu_ref[...], preferred_element_type=F32)
    return (g * jax.nn.sigmoid(g) * u).astype(BF16)


def _ffn_up_kernel(x_ref, wg_ref, wu_ref, *rest, has_tail, n_side):
    rest = list(rest)
    wgt_ref, wut_ref = (rest.pop(0), rest.pop(0)) if has_tail else (None, None)
    side_srcs = [rest.pop(0) for _ in range(n_side)]
    o_ref = rest.pop(0)
    ot_ref = rest.pop(0) if has_tail else None
    side_dsts = [rest.pop(0) for _ in range(n_side)]
    if n_side:
        step = pl.program_id(0) * pl.num_programs(1) + pl.program_id(1)
        _side_cast_step(step, side_srcs, side_dsts, *rest)
    o_ref[...] = _swiglu_tile(x_ref[...], wg_ref, wu_ref)
    if has_tail:
        @pl.when(pl.program_id(1) == 0)
        def _():
            ot_ref[...] = _swiglu_tile(x_ref[...], wgt_ref, wut_ref)


def _ffn_up(x, w_gate, w_up, side=(), *, tm, tn, side_rows=None):
    t, k = x.shape
    n_main, n_tail = _split_tail(w_gate.shape[1], tn)
    assert t % tm == 0
    grid = (t // tm, n_main // tn)
    side_specs, side_out, side_scratch = _side_cast_plan(side, side_rows, grid[0] * grid[1])
    w_spec = pl.BlockSpec((k, tn), lambda i, j: (0, j))
    in_specs = [pl.BlockSpec((tm, k), lambda i, j: (i, 0)), w_spec, w_spec]
    out_specs = [pl.BlockSpec((None, tm, tn), lambda i, j: (j, i, 0))]
    out_shape = [jax.ShapeDtypeStruct((n_main // tn, t, tn), BF16)]
    operands = [x, w_gate, w_up]
    blocks = [((tm, k), x.dtype), ((k, tn), BF16), ((k, tn), BF16), ((tm, tn), BF16)]
    if n_tail:
        tail_spec = pl.BlockSpec((k, n_tail), lambda i, j: (0, n_main // n_tail),
                                 pipeline_mode=pl.Buffered(1))
        in_specs += [tail_spec, tail_spec]
        out_specs.append(pl.BlockSpec((tm, n_tail), lambda i, j: (i, 0)))
        out_shape.append(jax.ShapeDtypeStruct((t, n_tail), BF16))
        operands += [w_gate, w_up]
        blocks += [((k, n_tail), BF16), ((tm, n_tail), BF16)]
    outs = pl.pallas_call(
        functools.partial(_ffn_up_kernel, has_tail=bool(n_tail), n_side=len(side)),
        grid=grid,
        in_specs=in_specs + side_specs,
        out_specs=out_specs + side_specs,
        out_shape=out_shape + side_out,
        scratch_shapes=_scratch_shapes(side_scratch),
        compiler_params=_params(("arbitrary", "arbitrary"),
                                _vmem_limit(blocks, side_scratch + [((tm, tn), F32)] * 4)),
        name="ffn_up",
    )(*operands, *side)
    n_out = 2 if n_tail else 1
    return outs[0], (outs[1] if n_tail else None), list(outs[n_out:])


def _w_in_kernel(x_ref, w_ref, vg_ref, *rest, n_u, n_v, n_side):
    side_srcs, rest = rest[:n_side], rest[n_side:]
    (u_ref, v_ref, p_ref), rest = rest[:3], rest[3:]
    side_dsts, scratch = rest[:n_side], rest[n_side:]
    j = pl.program_id(1)
    if n_side:
        _side_cast_step(pl.program_id(0) * pl.num_programs(1) + j, side_srcs, side_dsts, *scratch)

    matmul = lambda: jnp.dot(x_ref[...], w_ref[...], preferred_element_type=F32)

    @pl.when(j < n_u)
    def _():
        u_ref[...] = _gelu(matmul()).astype(u_ref.dtype)

    @pl.when(jnp.logical_and(j >= n_u, j < n_u + n_v))
    def _():
        g = _gelu(matmul())
        for hd in range(v_ref.shape[1] // HEAD_DIM):
            cols = slice(hd * HEAD_DIM, (hd + 1) * HEAD_DIM)
            gh = g[:, cols]
            v_ref[:, cols] = (gh * _rms_scale(gh) * vg_ref[:, cols]).astype(v_ref.dtype)

    @pl.when(j >= n_u + n_v)
    def _():
        p_ref[...] = matmul()


def _w_in(x, w, v_gain, side=(), *, tm, tn, side_rows=None):
    t, k = x.shape
    assert t % tm == 0 and D_A % tn == 0 and D_B % tn == 0 and w.shape[1] == 2 * D_A + D_B
    n_u = n_v = D_A // tn
    n_p = D_B // tn
    grid = (t // tm, n_u + n_v + n_p)
    clamp = lambda j, lo, cnt: jnp.clip(j - lo, 0, cnt - 1)
    side_specs, side_out, side_scratch = _side_cast_plan(side, side_rows, grid[0] * grid[1])
    blocks = [((tm, k), x.dtype), ((k, tn), BF16), ((tm, tn), BF16), ((tm, tn), BF16), ((tm, tn), F32)]
    outs = pl.pallas_call(
        functools.partial(_w_in_kernel, n_u=n_u, n_v=n_v, n_side=len(side)),
        grid=grid,
        in_specs=[pl.BlockSpec((tm, k), lambda i, j: (i, 0)),
                  pl.BlockSpec((k, tn), lambda i, j: (0, j)),
                  pl.BlockSpec((1, tn), lambda i, j: (0, clamp(j, n_u, n_v)))] + side_specs,
        out_specs=[pl.BlockSpec((tm, tn), lambda i, j: (i, clamp(j, 0, n_u))),
                   pl.BlockSpec((tm, tn), lambda i, j: (i, clamp(j, n_u, n_v))),
                   pl.BlockSpec((tm, tn), lambda i, j: (i, clamp(j, n_u + n_v, n_p)))] + side_specs,
        out_shape=[jax.ShapeDtypeStruct((t, D_A), BF16), jax.ShapeDtypeStruct((t, D_A), BF16),
                   jax.ShapeDtypeStruct((t, D_B), F32)] + side_out,
        scratch_shapes=_scratch_shapes(side_scratch),
        compiler_params=_params(("arbitrary", "arbitrary"),
                                _vmem_limit(blocks, side_scratch + [((tm, tn), F32)] * 3)),
        name="w_in",
    )(x, w, v_gain, *side)
    return outs[0], outs[1], outs[2], list(outs[3:])


EPILOGUE_ROWS = 128
STAGE_SLOTS = 4


def _down_kernel(*refs, scale, emit_residual, has_tail, chunk):
    refs = list(refs)
    xm_ref = refs.pop(0)
    xt_ref = refs.pop(0) if has_tail else None
    wm_ref = refs.pop(0)
    wt_ref = refs.pop(0) if has_tail else None
    gp_ref, gn_ref, h_hbm = refs.pop(0), refs.pop(0), refs.pop(0)
    hout_hbm = refs.pop(0) if emit_residual else None
    nxt_hbm = refs.pop(0)
    y_scr, ss_scr, h_scr = refs.pop(0), refs.pop(0), refs.pop(0)
    stage = refs.pop(0) if emit_residual else None
    (sems,) = refs

    i, k = pl.program_id(0), pl.program_id(1)
    n_i, n_k = pl.num_programs(0), pl.num_programs(1)
    tm, n_cols = y_scr.shape
    col_chunks = [slice(c * chunk, (c + 1) * chunk) for c in range(n_cols // chunk)]
    n_slots = stage.shape[0] if emit_residual else 0
    rows = lambda p: pl.ds(p * tm, tm)

    def fetch(p):
        return pltpu.make_async_copy(h_hbm.at[rows(p), :], h_scr, sems.at[0])

    def store_panel(p):
        dst = hout_hbm if emit_residual else nxt_hbm
        return pltpu.make_async_copy(h_scr, dst.at[rows(p), :], sems.at[1])

    rb = stage.shape[1] if emit_residual else EPILOGUE_ROWS
    row_blocks = [slice(b * rb, (b + 1) * rb) for b in range(tm // rb)]

    def store_stage(p, b):
        slot = b % n_slots
        return pltpu.make_async_copy(stage.at[slot], nxt_hbm.at[pl.ds(p * tm + b * rb, rb), :],
                                     sems.at[2 + slot])

    def outstanding(p):
        last = range(len(row_blocks) - n_slots, len(row_blocks))
        return [store_panel(p)] + [store_stage(p, b) for b in last]

    def partial_y(cols):
        return jnp.dot(xm_ref[...], wm_ref[:, cols], preferred_element_type=F32)

    def first_y(cols):
        y = partial_y(cols)
        if has_tail:
            y = y + jnp.dot(xt_ref[...], wt_ref[:, cols], preferred_element_type=F32)
        return y

    def row_scale(ss):
        return lax.rsqrt(jnp.sum(ss, axis=-1, keepdims=True) * (1.0 / n_cols) + EPS)

    def epilogue_rows(p, b):
        rws = row_blocks[b]
        r1 = scale * row_scale(ss_scr[rws, :])
        sq = jnp.zeros((rb, V7X_LANES), F32)
        for cols in col_chunks:
            h_new = h_scr[rws, cols] + y_scr[rws, cols] * r1 * gp_ref[:, cols]
            h_scr[rws, cols] = h_new
            sq = sq + _lane_fold(h_new * h_new)
        r2 = row_scale(sq)
        if emit_residual and b >= n_slots:
            store_stage(p, b - n_slots).wait()
        for cols in col_chunks:
            nxt = h_scr[rws, cols] * r2 * gn_ref[:, cols]
            if emit_residual:
                stage[b % n_slots, :, cols] = nxt.astype(stage.dtype)
            else:
                h_scr[rws, cols] = nxt
        if emit_residual:
            store_stage(p, b).start(priority=PANEL_DMA_PRIORITY)

    has_prev = i > 0

    @pl.when(k == 0)
    def _():
        for cols in col_chunks:
            y_scr[:, cols] = first_y(cols)

    @pl.when(jnp.logical_and(k > 0, k < n_k - 1))
    def _():
        for cols in col_chunks:
            y_scr[:, cols] += partial_y(cols)

    @pl.when(k == n_k // 2)
    def _():
        @pl.when(has_prev)
        def _():
            for copy in outstanding(i - 1):
                copy.wait()
        fetch(i).start(priority=PANEL_DMA_PRIORITY)

    @pl.when(k == n_k - 1)
    def _():
        sq = jnp.zeros((tm, V7X_LANES), F32)
        for cols in col_chunks:
            y = y_scr[:, cols] + partial_y(cols)
            y_scr[:, cols] = y
            sq = sq + _lane_fold(y * y)
        ss_scr[...] = sq

    @pl.when(k == n_k - 1)
    def _():
        fetch(i).wait()
        for b in range(len(row_blocks)):
            epilogue_rows(i, b)
        store_panel(i).start(priority=PANEL_DMA_PRIORITY)

        @pl.when(i == n_i - 1)
        def _():
            for copy in outstanding(i):
                copy.wait()


def _down_residual(x_main, x_tail, w, h, gain_post, gain_next, *, scale, emit_residual,
                   tm, tk, chunk, name):
    if x_main.ndim == 3:
        n_k, t, _ = x_main.shape
        assert x_main.shape[2] == tk
        k_main = n_k * tk
        x_spec = pl.BlockSpec((None, tm, tk), lambda i, k: (k, i, 0))
    else:
        t, k_main = x_main.shape
        n_k = k_main // tk
        x_spec = pl.BlockSpec((tm, tk), lambda i, k: (i, k))
    k_tail = 0 if x_tail is None else x_tail.shape[1]
    n = w.shape[1]
    assert t % tm == 0 and tm % EPILOGUE_ROWS == 0 and k_main % tk == 0 and n % chunk == 0
    assert n_k >= 4 and h.shape == (t, n) and w.shape[0] == k_main + k_tail
    assert k_tail == 0 or k_main % k_tail == 0

    any_spec = pl.BlockSpec(memory_space=pl.ANY)
    once = dict(pipeline_mode=pl.Buffered(1))
    gain_spec = pl.BlockSpec((1, n), lambda i, k: (0, 0), **once)
    in_specs = [x_spec]
    operands = [x_main]
    blocks = [((tm, tk), BF16), ((tk, n), BF16)]
    if k_tail:
        in_specs.append(pl.BlockSpec((tm, k_tail), lambda i, k: (i, 0)))
        operands.append(x_tail)
        blocks.append(((tm, k_tail), BF16))
    in_specs.append(pl.BlockSpec((tk, n), lambda i, k: (k, 0)))
    operands.append(w)
    if k_tail:
        in_specs.append(pl.BlockSpec((k_tail, n), lambda i, k: (k_main // k_tail, 0), **once))
        operands.append(w)
    in_specs += [gain_spec, gain_spec, any_spec]
    operands += [gain_post, gain_next, h]

    out_shape = [jax.ShapeDtypeStruct((t, n), BF16 if emit_residual else F32)]
    scratch = [((tm, n), F32), ((tm, V7X_LANES), F32), ((tm, n), F32)]
    n_sems = 2
    if emit_residual:
        out_shape.insert(0, jax.ShapeDtypeStruct((t, n), F32))
        n_slots = min(STAGE_SLOTS, tm // EPILOGUE_ROWS)
        scratch.append(((n_slots, EPILOGUE_ROWS, n), BF16))
        n_sems = 2 + n_slots
    single = scratch + [((k_tail, n), BF16), ((2, n), F32), ((tm, chunk), F32), ((tm, chunk), F32)]
    return pl.pallas_call(
        functools.partial(_down_kernel, scale=scale, emit_residual=emit_residual,
                          has_tail=bool(k_tail), chunk=chunk),
        grid=(t // tm, n_k),
        in_specs=in_specs,
        out_specs=[any_spec] * len(out_shape),
        out_shape=out_shape,
        scratch_shapes=[pltpu.VMEM(s, d) for s, d in scratch] + [pltpu.SemaphoreType.DMA((n_sems,))],
        compiler_params=_params(("arbitrary", "arbitrary"), _vmem_limit(blocks, single)),
        name=name,
    )(*operands)


def _mixer_kernel(u_ref, v_ref, p_ref, pprev_ref, pnext_ref, ws_ref, bs_ref, wp_ref, ps_ref,
                  o_ref, *, tiles_per_seq, seq_len):
    tm = u_ref.shape[0]
    n_chunks = tm // CHUNK
    panel = o_ref.shape[2]

    def put(rows, col, value):
        off = col % panel
        o_ref[col // panel, rows, off:off + value.shape[1]] = value.astype(o_ref.dtype)

    for hd in range(N_HEADS):
        cols = slice(hd * HEAD_DIM, (hd + 1) * HEAD_DIM)
        v_h = jnp.concatenate(
            [v_ref[c * CHUNK:(c + 1) * CHUNK, cols] for c in range(n_chunks)], axis=1)
        sv = jnp.dot(ws_ref[hd], v_h, preferred_element_type=F32)
        bias = bs_ref[hd]
        for c in range(n_chunks):
            rows = slice(c * CHUNK, (c + 1) * CHUNK)
            gate = sv[:, c * HEAD_DIM:(c + 1) * HEAD_DIM] + bias
            put(rows, hd * HEAD_DIM, u_ref[rows, cols].astype(F32) * gate)

    pos = pl.program_id(0) % tiles_per_seq
    prev = jnp.where(pos > 0, pprev_ref[...], 0.0)
    nxt = jnp.where(pos < tiles_per_seq - 1, pnext_ref[...], 0.0)
    row = lax.broadcasted_iota(jnp.int32, (tm, 1), 0) + pos * tm
    for g, w in enumerate(POOL_WINDOWS):
        half = w // 2
        cols = slice(g * D_POOL_GROUP, (g + 1) * D_POOL_GROUP)
        p = p_ref[:, cols]
        s = jnp.concatenate([prev[:, cols], p, nxt[:, cols]], axis=0)
        step = 1
        while step < w:
            s = s[:s.shape[0] - step] + s[step:]
            step *= 2
        win = s[POOL_HALO - half:POOL_HALO - half + tm]
        count = jnp.minimum(row + half, seq_len) - jnp.maximum(row - half, 0)
        pooled = win * (1.0 / count.astype(F32)) - p
        y_b = jnp.dot(pooled.astype(BF16), wp_ref[g], preferred_element_type=F32) * ps_ref[:, cols]
        put(slice(None), D_A + g * D_POOL_GROUP, y_b)


def _mixer(u, v, p, w_spatial, b_spatial, w_pool, pool_scale, *, seq_len, panel, tm=512):
    t = u.shape[0]
    assert seq_len % tm == 0 and tm % CHUNK == 0 and panel % HEAD_DIM == 0 and panel % D_POOL_GROUP == 0
    halo_blocks_per_tile = tm // POOL_HALO
    n_halo_blocks = t // POOL_HALO
    row_spec = lambda width: pl.BlockSpec((tm, width), lambda i: (i, 0))
    const3 = lambda shape: pl.BlockSpec(shape, lambda i: (0, 0, 0))
    blocks = [((tm, D_A), BF16)] * 2 + [((tm, D_B), F32), ((tm, D_MODEL), BF16),
              (w_spatial.shape, BF16), (b_spatial.shape, F32), (w_pool.shape, BF16)]
    return pl.pallas_call(
        functools.partial(_mixer_kernel, tiles_per_seq=seq_len // tm, seq_len=seq_len),
        grid=(t // tm,),
        in_specs=[
            row_spec(D_A), row_spec(D_A), row_spec(D_B),
            pl.BlockSpec((POOL_HALO, D_B),
                         lambda i: (jnp.maximum(i * halo_blocks_per_tile - 1, 0), 0)),
            pl.BlockSpec((POOL_HALO, D_B),
                         lambda i: (jnp.minimum((i + 1) * halo_blocks_per_tile, n_halo_blocks - 1), 0)),
            const3(w_spatial.shape), const3(b_spatial.shape), const3(w_pool.shape),
            pl.BlockSpec((1, D_B), lambda i: (0, 0)),
        ],
        out_specs=pl.BlockSpec((D_MODEL // panel, tm, panel), lambda i: (0, i, 0)),
        out_shape=jax.ShapeDtypeStruct((D_MODEL // panel, t, panel), BF16),
        compiler_params=_params(("parallel",),
                                _vmem_limit(blocks, [((tm + 2 * POOL_HALO, D_B), F32)] * 3)),
        name="token_mixer",
    )(u, v, p, p, p, w_spatial, b_spatial, w_pool, pool_scale)


DOWN_TK = 512
_FFN_UP = functools.partial(_ffn_up, tm=1024, tn=DOWN_TK)
_DOWN = functools.partial(_down_residual, tm=1024, tk=DOWN_TK, chunk=512)
_W_IN = functools.partial(_w_in, tm=1024, tn=512)


def _as_rows(x):
    b, s, d = x.shape
    return x.reshape(b * s, d)


def _mix_out(u, v, p, h1, w, seq_len):
    y_cat = _mixer(u, v, p, w["w_spatial"], w["b_spatial"], w["w_pool"], w["pool_scale"],
                   seq_len=seq_len, panel=DOWN_TK)
    return _DOWN(y_cat, None, w["w_out"], h1, w["mix_post"], w["ffn2_pre"],
                 scale=1.0, emit_residual=True, name="w_out")


def _ffn2(h2, xn, w, out_shape):
    act, act_tail, _ = _FFN_UP(xn, w["ffn2_gate"], w["ffn2_up"])
    (out,) = _DOWN(act, act_tail, w["ffn2_down"], h2, w["ffn2_post"], w["final_norm"],
                   scale=FFN_RESIDUAL, emit_residual=False, name="ffn2_down")
    return out.reshape(out_shape)


def kernel(x_prompt, x_sample, ffn1_pre, ffn1_post, ffn1_gate, ffn1_up, ffn1_down, mix_pre, mix_post, w_in, w_spatial, b_spatial, v_norm, w_pool, pool_scale, w_out, ffn2_pre, ffn2_post, ffn2_gate, ffn2_up, ffn2_down, final_norm):
    assert ffn1_pre.shape[0] == 1, "single-layer trunk"
    row = lambda g: g[0].reshape(1, -1).astype(F32)
    mxu = lambda m: m[0].astype(BF16)
    raw = lambda m: m[0].astype(F32)
    w = {
        "ffn1_pre": row(ffn1_pre), "ffn1_post": row(ffn1_post),
        "ffn1_gate": mxu(ffn1_gate), "ffn1_up": mxu(ffn1_up),
        "mix_pre": row(mix_pre), "mix_post": row(mix_post),
        "w_spatial": mxu(w_spatial),
        "b_spatial": jnp.broadcast_to(b_spatial[0][:, :, None], (N_HEADS, CHUNK, HEAD_DIM)).astype(F32),
        "v_norm": row(v_norm), "w_pool": mxu(w_pool), "pool_scale": row(pool_scale),
        "ffn2_pre": row(ffn2_pre), "ffn2_post": row(ffn2_post),
        "final_norm": row(final_norm),
    }
    h0_p, h0_s = _as_rows(x_prompt), _as_rows(x_sample)
    act_p, tail_p, (w["ffn1_down"],) = _FFN_UP(
        _rmsnorm_rows(h0_p, w["ffn1_pre"]), w["ffn1_gate"], w["ffn1_up"], (raw(ffn1_down),),
        side_rows=FFN1_DOWN_CAST_ROWS)
    act_s, tail_s, (w["w_in"],) = _FFN_UP(
        _rmsnorm_rows(h0_s, w["ffn1_pre"]), w["ffn1_gate"], w["ffn1_up"], (raw(w_in),),
        side_rows=W_IN_CAST_ROWS)
    ffn1_down_args = dict(scale=FFN_RESIDUAL, emit_residual=True, name="ffn1_down")
    h1_p, a_p = _DOWN(act_p, tail_p, w["ffn1_down"], h0_p, w["ffn1_post"], w["mix_pre"], **ffn1_down_args)
    h1_s, a_s = _DOWN(act_s, tail_s, w["ffn1_down"], h0_s, w["ffn1_post"], w["mix_pre"], **ffn1_down_args)

    u_s, v_s, p_s, (w["ffn2_down"], w["w_out"]) = _W_IN(
        a_s, w["w_in"], w["v_norm"], (raw(ffn2_down), raw(w_out)), side_rows=SQUARE_CAST_ROWS)
    u_p, v_p, p_p, (w["ffn2_gate"], w["ffn2_up"]) = _W_IN(
        a_p, w["w_in"], w["v_norm"], (raw(ffn2_gate), raw(ffn2_up)), side_rows=FFN_UP_CAST_ROWS)
    h2_p, xn_p = _mix_out(u_p, v_p, p_p, h1_p, w, x_prompt.shape[1])
    h2_s, xn_s = _mix_out(u_s, v_s, p_s, h1_s, w, x_sample.shape[1])
    return _ffn2(h2_p, xn_p, w, x_prompt.shape), _ffn2(h2_s, xn_s, w, x_sample.shape)
```

```python
import functools
import math

import jax
import jax.numpy as jnp
from jax import lax
from jax.experimental import pallas as pl
from jax.experimental.pallas import tpu as pltpu

F32 = jnp.float32
BF16 = jnp.bfloat16

D_MODEL = 4096
D_A = D_MODEL // 2
HEAD_DIM = 128
N_HEADS = D_A // HEAD_DIM
CHUNK = 128
D_B = D_MODEL - D_A
POOL_WINDOWS = (2, 4, 8, 16)
D_POOL_GROUP = D_B // len(POOL_WINDOWS)
POOL_HALO = max(POOL_WINDOWS) // 2
EPS = 1e-6
FFN_RESIDUAL = 0.5

V7X_LANES = 128
V7X_VMEM_BYTES = 64 * 1024 * 1024
COMPILER_SCRATCH_BYTES = 4 * 1024 * 1024
PANEL_DMA_PRIORITY = 1


def _nbytes(shape, dtype):
    return math.prod(shape) * jnp.dtype(dtype).itemsize


def _vmem_limit(pipelined_blocks, temporaries=()):
    need = 2 * sum(_nbytes(s, d) for s, d in pipelined_blocks)
    need += sum(_nbytes(s, d) for s, d in temporaries)
    need += COMPILER_SCRATCH_BYTES
    assert need <= V7X_VMEM_BYTES, need
    return need


def _params(semantics, vmem_bytes):
    return pltpu.CompilerParams(dimension_semantics=semantics, vmem_limit_bytes=vmem_bytes)


def _rms_scale(x):
    return lax.rsqrt(jnp.mean(x * x, axis=-1, keepdims=True) + EPS)


def _gelu(x):
    return 0.5 * x * (1.0 + lax.erf(x * math.sqrt(0.5)))


def _lane_fold(x):
    acc = x[:, :V7X_LANES]
    for c in range(1, x.shape[1] // V7X_LANES):
        acc = acc + x[:, c * V7X_LANES:(c + 1) * V7X_LANES]
    return acc


def _split_tail(n, tile):
    n_tail = n % tile
    n_main = n - n_tail
    assert n_tail % V7X_LANES == 0 and (n_tail == 0 or n_main % n_tail == 0)
    return n_main, n_tail


def _norm_kernel(x_ref, g_ref, o_ref):
    x = x_ref[...]
    o_ref[...] = (x * _rms_scale(x) * g_ref[...]).astype(o_ref.dtype)


def _rmsnorm_rows(x, gain, *, rows=512):
    t, d = x.shape
    blocks = [((rows, d), F32), ((rows, d), BF16)]
    return pl.pallas_call(
        _norm_kernel,
        grid=(t // rows,),
        in_specs=[pl.BlockSpec((rows, d), lambda i: (i, 0)),
                  pl.BlockSpec((1, d), lambda i: (0, 0))],
        out_specs=pl.BlockSpec((rows, d), lambda i: (i, 0)),
        out_shape=jax.ShapeDtypeStruct((t, d), BF16),
        compiler_params=_params(("parallel",), _vmem_limit(blocks, [((rows, d), F32)] * 2)),
        name="rmsnorm_rows",
    )(x, gain)


FFN1_DOWN_CAST_ROWS = 64
W_IN_CAST_ROWS = 32
FFN_UP_CAST_ROWS = 64
SQUARE_CAST_ROWS = 256


def _side_cast_step(step, srcs, dsts, f32_stage, bf16_stage, sems):
    rows = f32_stage.shape[1]
    starts = [0]
    for src in srcs:
        starts.append(starts[-1] + src.shape[0] // rows)
    n_blocks = starts[-1]

    def per_array(blk, fn):
        for a, (src, dst) in enumerate(zip(srcs, dsts)):
            @pl.when(jnp.logical_and(blk >= starts[a], blk < starts[a + 1]))
            def _():
                fn(src, dst, pl.ds((blk - starts[a]) * rows, rows))

    def load(src, block_rows, blk):
        return pltpu.make_async_copy(src.at[block_rows, :], f32_stage.at[blk % 2], sems.at[blk % 2])

    def store(dst, block_rows, blk):
        return pltpu.make_async_copy(bf16_stage.at[blk % 2], dst.at[block_rows, :], sems.at[2 + blk % 2])

    first_rows = pl.ds(0, rows)

    @pl.when(step == 0)
    def _():
        load(srcs[0], first_rows, step).start(priority=PANEL_DMA_PRIORITY)

    @pl.when(step + 1 < n_blocks)
    def _():
        per_array(step + 1, lambda src, dst, r: load(src, r, step + 1).start(priority=PANEL_DMA_PRIORITY))

    @pl.when(step < n_blocks)
    def _():
        load(srcs[0], first_rows, step).wait()

        @pl.when(step >= 2)
        def _():
            store(dsts[0], first_rows, step - 2).wait()
        bf16_stage[step % 2] = f32_stage[step % 2].astype(bf16_stage.dtype)
        per_array(step, lambda src, dst, r: store(dst, r, step).start(priority=PANEL_DMA_PRIORITY))

    @pl.when(step == n_blocks - 1)
    def _():
        if n_blocks > 1:
            store(dsts[0], first_rows, step - 1).wait()
        store(dsts[0], first_rows, step).wait()


def _side_cast_plan(side, side_rows, n_steps):
    if not side:
        return [], [], []
    cols = side[0].shape[1]
    assert all(s.shape[1] == cols and s.dtype == F32 and s.shape[0] % side_rows == 0 for s in side)
    assert sum(s.shape[0] // side_rows for s in side) <= n_steps
    any_specs = [pl.BlockSpec(memory_space=pl.ANY)] * len(side)
    out_shape = [jax.ShapeDtypeStruct(s.shape, BF16) for s in side]
    scratch = [((2, side_rows, cols), F32), ((2, side_rows, cols), BF16)]
    return any_specs, out_shape, scratch


def _scratch_shapes(side_scratch):
    if not side_scratch:
        return []
    return [pltpu.VMEM(s, d) for s, d in side_scratch] + [pltpu.SemaphoreType.DMA((4,))]


def _swiglu_tile(x, wg_ref, wu_ref):
    g = jnp.dot(x, wg_ref[...], preferred_element_type=F32)
    u = jnp.dot(x, wu_ref[...], preferred_element_type=F32)
    return (g * jax.nn.sigmoid(g) * u).astype(BF16)


def _ffn_up_kernel(x_ref, wg_ref, wu_ref, *rest, has_tail, n_side):
    rest = list(rest)
    wgt_ref, wut_ref = (rest.pop(0), rest.pop(0)) if has_tail else (None, None)
    side_srcs = [rest.pop(0) for _ in range(n_side)]
    o_ref = rest.pop(0)
    ot_ref = rest.pop(0) if has_tail else None
    side_dsts = [rest.pop(0) for _ in range(n_side)]
    if n_side:
        step = pl.program_id(0) * pl.num_programs(1) + pl.program_id(1)
        _side_cast_step(step, side_srcs, side_dsts, *rest)
    o_ref[...] = _swiglu_tile(x_ref[...], wg_ref, wu_ref)
    if has_tail:
        @pl.when(pl.program_id(1) == 0)
        def _():
            ot_ref[...] = _swiglu_tile(x_ref[...], wgt_ref, wut_ref)


def _ffn_up(x, w_gate, w_up, side=(), *, tm, tn, side_rows=None):
    t, k = x.shape
    n_main, n_tail = _split_tail(w_gate.shape[1], tn)
    assert t % tm == 0
    grid = (t // tm, n_main // tn)
    side_specs, side_out, side_scratch = _side_cast_plan(side, side_rows, grid[0] * grid[1])
    w_spec = pl.BlockSpec((k, tn), lambda i, j: (0, j))
    in_specs = [pl.BlockSpec((tm, k), lambda i, j: (i, 0)), w_spec, w_spec]
    out_specs = [pl.BlockSpec((tm, tn), lambda i, j: (i, j))]
    out_shape = [jax.ShapeDtypeStruct((t, n_main), BF16)]
    operands = [x, w_gate, w_up]
    blocks = [((tm, k), x.dtype), ((k, tn), BF16), ((k, tn), BF16), ((tm, tn), BF16)]
    if n_tail:
        tail_spec = pl.BlockSpec((k, n_tail), lambda i, j: (0, n_main // n_tail),
                                 pipeline_mode=pl.Buffered(1))
        in_specs += [tail_spec, tail_spec]
        out_specs.append(pl.BlockSpec((tm, n_tail), lambda i, j: (i, 0)))
        out_shape.append(jax.ShapeDtypeStruct((t, n_tail), BF16))
        operands += [w_gate, w_up]
        blocks += [((k, n_tail), BF16), ((tm, n_tail), BF16)]
    outs = pl.pallas_call(
        functools.partial(_ffn_up_kernel, has_tail=bool(n_tail), n_side=len(side)),
        grid=grid,
        in_specs=in_specs + side_specs,
        out_specs=out_specs + side_specs,
        out_shape=out_shape + side_out,
        scratch_shapes=_scratch_shapes(side_scratch),
        compiler_params=_params(("arbitrary", "arbitrary"),
                                _vmem_limit(blocks, side_scratch + [((tm, tn), F32)] * 4)),
        name="ffn_up",
    )(*operands, *side)
    n_out = 2 if n_tail else 1
    return outs[0], (outs[1] if n_tail else None), list(outs[n_out:])


def _w_in_kernel(x_ref, w_ref, vg_ref, *rest, n_u, n_v, n_side):
    side_srcs, rest = rest[:n_side], rest[n_side:]
    (u_ref, v_ref, p_ref), rest = rest[:3], rest[3:]
    side_dsts, scratch = rest[:n_side], rest[n_side:]
    j = pl.program_id(1)
    if n_side:
        _side_cast_step(pl.program_id(0) * pl.num_programs(1) + j, side_srcs, side_dsts, *scratch)

    matmul = lambda: jnp.dot(x_ref[...], w_ref[...], preferred_element_type=F32)

    @pl.when(j < n_u)
    def _():
        u_ref[...] = _gelu(matmul()).astype(u_ref.dtype)

    @pl.when(jnp.logical_and(j >= n_u, j < n_u + n_v))
    def _():
        g = _gelu(matmul())
        for hd in range(v_ref.shape[1] // HEAD_DIM):
            cols = slice(hd * HEAD_DIM, (hd + 1) * HEAD_DIM)
            gh = g[:, cols]
            v_ref[:, cols] = (gh * _rms_scale(gh) * vg_ref[:, cols]).astype(v_ref.dtype)

    @pl.when(j >= n_u + n_v)
    def _():
        p_ref[...] = matmul()


def _w_in(x, w, v_gain, side=(), *, tm, tn, side_rows=None):
    t, k = x.shape
    assert t % tm == 0 and D_A % tn == 0 and D_B % tn == 0 and w.shape[1] == 2 * D_A + D_B
    n_u = n_v = D_A // tn
    n_p = D_B // tn
    grid = (t // tm, n_u + n_v + n_p)
    clamp = lambda j, lo, cnt: jnp.clip(j - lo, 0, cnt - 1)
    side_specs, side_out, side_scratch = _side_cast_plan(side, side_rows, grid[0] * grid[1])
    blocks = [((tm, k), x.dtype), ((k, tn), BF16), ((tm, tn), BF16), ((tm, tn), BF16), ((tm, tn), F32)]
    outs = pl.pallas_call(
        functools.partial(_w_in_kernel, n_u=n_u, n_v=n_v, n_side=len(side)),
        grid=grid,
        in_specs=[pl.BlockSpec((tm, k), lambda i, j: (i, 0)),
                  pl.BlockSpec((k, tn), lambda i, j: (0, j)),
                  pl.BlockSpec((1, tn), lambda i, j: (0, clamp(j, n_u, n_v)))] + side_specs,
        out_specs=[pl.BlockSpec((tm, tn), lambda i, j: (i, clamp(j, 0, n_u))),
                   pl.BlockSpec((tm, tn), lambda i, j: (i, clamp(j, n_u, n_v))),
                   pl.BlockSpec((tm, tn), lambda i, j: (i, clamp(j, n_u + n_v, n_p)))] + side_specs,
        out_shape=[jax.ShapeDtypeStruct((t, D_A), BF16), jax.ShapeDtypeStruct((t, D_A), BF16),
                   jax.ShapeDtypeStruct((t, D_B), F32)] + side_out,
        scratch_shapes=_scratch_shapes(side_scratch),
        compiler_params=_params(("arbitrary", "arbitrary"),
                                _vmem_limit(blocks, side_scratch + [((tm, tn), F32)] * 3)),
        name="w_in",
    )(x, w, v_gain, *side)
    return outs[0], outs[1], outs[2], list(outs[3:])


EPILOGUE_ROWS = 128
STAGE_SLOTS = 4


def _down_kernel(*refs, scale, emit_residual, has_tail, chunk):
    refs = list(refs)
    xm_ref = refs.pop(0)
    xt_ref = refs.pop(0) if has_tail else None
    wm_ref = refs.pop(0)
    wt_ref = refs.pop(0) if has_tail else None
    gp_ref, gn_ref, h_hbm = refs.pop(0), refs.pop(0), refs.pop(0)
    hout_hbm = refs.pop(0) if emit_residual else None
    nxt_hbm = refs.pop(0)
    y_scr, ss_scr, h_scr = refs.pop(0), refs.pop(0), refs.pop(0)
    stage = refs.pop(0) if emit_residual else None
    (sems,) = refs

    i, k = pl.program_id(0), pl.program_id(1)
    n_i, n_k = pl.num_programs(0), pl.num_programs(1)
    tm, n_cols = y_scr.shape
    col_chunks = [slice(c * chunk, (c + 1) * chunk) for c in range(n_cols // chunk)]
    n_slots = stage.shape[0] if emit_residual else 0
    rows = lambda p: pl.ds(p * tm, tm)

    def fetch(p):
        return pltpu.make_async_copy(h_hbm.at[rows(p), :], h_scr, sems.at[0])

    def store_panel(p):
        dst = hout_hbm if emit_residual else nxt_hbm
        return pltpu.make_async_copy(h_scr, dst.at[rows(p), :], sems.at[1])

    rb = stage.shape[1] if emit_residual else EPILOGUE_ROWS
    row_blocks = [slice(b * rb, (b + 1) * rb) for b in range(tm // rb)]

    def store_stage(p, b):
        slot = b % n_slots
        return pltpu.make_async_copy(stage.at[slot], nxt_hbm.at[pl.ds(p * tm + b * rb, rb), :],
                                     sems.at[2 + slot])

    def outstanding(p):
        last = range(len(row_blocks) - n_slots, len(row_blocks))
        return [store_panel(p)] + [store_stage(p, b) for b in last]

    def partial_y(cols):
        return jnp.dot(xm_ref[...], wm_ref[:, cols], preferred_element_type=F32)

    def first_y(cols):
        y = partial_y(cols)
        if has_tail:
            y = y + jnp.dot(xt_ref[...], wt_ref[:, cols], preferred_element_type=F32)
        return y

    def row_scale(ss):
        return lax.rsqrt(jnp.sum(ss, axis=-1, keepdims=True) * (1.0 / n_cols) + EPS)

    def epilogue_rows(p, b):
        rws = row_blocks[b]
        r1 = scale * row_scale(ss_scr[rws, :])
        sq = jnp.zeros((rb, V7X_LANES), F32)
        for cols in col_chunks:
            h_new = h_scr[rws, cols] + y_scr[rws, cols] * r1 * gp_ref[:, cols]
            h_scr[rws, cols] = h_new
            sq = sq + _lane_fold(h_new * h_new)
        r2 = row_scale(sq)
        if emit_residual and b >= n_slots:
            store_stage(p, b - n_slots).wait()
        for cols in col_chunks:
            nxt = h_scr[rws, cols] * r2 * gn_ref[:, cols]
            if emit_residual:
                stage[b % n_slots, :, cols] = nxt.astype(stage.dtype)
            else:
                h_scr[rws, cols] = nxt
        if emit_residual:
            store_stage(p, b).start(priority=PANEL_DMA_PRIORITY)

    has_prev = i > 0

    @pl.when(k == 0)
    def _():
        for cols in col_chunks:
            y_scr[:, cols] = first_y(cols)

    @pl.when(jnp.logical_and(k > 0, k < n_k - 1))
    def _():
        for cols in col_chunks:
            y_scr[:, cols] += partial_y(cols)

    @pl.when(k == n_k // 2)
    def _():
        @pl.when(has_prev)
        def _():
            for copy in outstanding(i - 1):
                copy.wait()
        fetch(i).start(priority=PANEL_DMA_PRIORITY)

    @pl.when(k == n_k - 1)
    def _():
        sq = jnp.zeros((tm, V7X_LANES), F32)
        for cols in col_chunks:
            y = y_scr[:, cols] + partial_y(cols)
            y_scr[:, cols] = y
            sq = sq + _lane_fold(y * y)
        ss_scr[...] = sq

    @pl.when(k == n_k - 1)
    def _():
        fetch(i).wait()
        for b in range(len(row_blocks)):
            epilogue_rows(i, b)
        store_panel(i).start(priority=PANEL_DMA_PRIORITY)

        @pl.when(i == n_i - 1)
        def _():
            for copy in outstanding(i):
                copy.wait()


def _down_residual(x_main, x_tail, w, h, gain_post, gain_next, *, scale, emit_residual,
                   tm, tk, chunk, name):
    if x_main.ndim == 3:
        n_k, t, _ = x_main.shape
        assert x_main.shape[2] == tk
        k_main = n_k * tk
        x_spec = pl.BlockSpec((None, tm, tk), lambda i, k: (k, i, 0))
    else:
        t, k_main = x_main.shape
        n_k = k_main // tk
        x_spec = pl.BlockSpec((tm, tk), lambda i, k: (i, k))
    k_tail = 0 if x_tail is None else x_tail.shape[1]
    n = w.shape[1]
    assert t % tm == 0 and tm % EPILOGUE_ROWS == 0 and k_main % tk == 0 and n % chunk == 0
    assert n_k >= 4 and h.shape == (t, n) and w.shape[0] == k_main + k_tail
    assert k_tail == 0 or k_main % k_tail == 0

    any_spec = pl.BlockSpec(memory_space=pl.ANY)
    once = dict(pipeline_mode=pl.Buffered(1))
    gain_spec = pl.BlockSpec((1, n), lambda i, k: (0, 0), **once)
    in_specs = [x_spec]
    operands = [x_main]
    blocks = [((tm, tk), BF16), ((tk, n), BF16)]
    if k_tail:
        in_specs.append(pl.BlockSpec((tm, k_tail), lambda i, k: (i, 0)))
        operands.append(x_tail)
        blocks.append(((tm, k_tail), BF16))
    in_specs.append(pl.BlockSpec((tk, n), lambda i, k: (k, 0)))
    operands.append(w)
    if k_tail:
        in_specs.append(pl.BlockSpec((k_tail, n), lambda i, k: (k_main // k_tail, 0), **once))
        operands.append(w)
    in_specs += [gain_spec, gain_spec, any_spec]
    operands += [gain_post, gain_next, h]

    out_shape = [jax.ShapeDtypeStruct((t, n), BF16 if emit_residual else F32)]
    scratch = [((tm, n), F32), ((tm, V7X_LANES), F32), ((tm, n), F32)]
    n_sems = 2
    if emit_residual:
        out_shape.insert(0, jax.ShapeDtypeStruct((t, n), F32))
        n_slots = min(STAGE_SLOTS, tm // EPILOGUE_ROWS)
        scratch.append(((n_slots, EPILOGUE_ROWS, n), BF16))
        n_sems = 2 + n_slots
    single = scratch + [((k_tail, n), BF16), ((2, n), F32), ((tm, chunk), F32), ((tm, chunk), F32)]
    return pl.pallas_call(
        functools.partial(_down_kernel, scale=scale, emit_residual=emit_residual,
                          has_tail=bool(k_tail), chunk=chunk),
        grid=(t // tm, n_k),
        in_specs=in_specs,
        out_specs=[any_spec] * len(out_shape),
        out_shape=out_shape,
        scratch_shapes=[pltpu.VMEM(s, d) for s, d in scratch] + [pltpu.SemaphoreType.DMA((n_sems,))],
        compiler_params=_params(("arbitrary", "arbitrary"), _vmem_limit(blocks, single)),
        name=name,
    )(*operands)


def _mixer_kernel(u_ref, v_ref, p_ref, pprev_ref, pnext_ref, ws_ref, bs_ref, wp_ref, ps_ref,
                  o_ref, *, tiles_per_seq, seq_len):
    tm = u_ref.shape[0]
    n_chunks = tm // CHUNK
    panel = o_ref.shape[2]

    def put(rows, col, value):
        off = col % panel
        o_ref[col // panel, rows, off:off + value.shape[1]] = value.astype(o_ref.dtype)

    for hd in range(N_HEADS):
        cols = slice(hd * HEAD_DIM, (hd + 1) * HEAD_DIM)
        v_h = jnp.concatenate(
            [v_ref[c * CHUNK:(c + 1) * CHUNK, cols] for c in range(n_chunks)], axis=1)
        sv = jnp.dot(ws_ref[hd], v_h, preferred_element_type=F32)
        bias = bs_ref[hd]
        for c in range(n_chunks):
            rows = slice(c * CHUNK, (c + 1) * CHUNK)
            gate = sv[:, c * HEAD_DIM:(c + 1) * HEAD_DIM] + bias
            put(rows, hd * HEAD_DIM, u_ref[rows, cols].astype(F32) * gate)

    pos = pl.program_id(0) % tiles_per_seq
    prev = jnp.where(pos > 0, pprev_ref[...], 0.0)
    nxt = jnp.where(pos < tiles_per_seq - 1, pnext_ref[...], 0.0)
    row = lax.broadcasted_iota(jnp.int32, (tm, 1), 0) + pos * tm
    for g, w in enumerate(POOL_WINDOWS):
        half = w // 2
        cols = slice(g * D_POOL_GROUP, (g + 1) * D_POOL_GROUP)
        p = p_ref[:, cols]
        s = jnp.concatenate([prev[:, cols], p, nxt[:, cols]], axis=0)
        step = 1
        while step < w:
            s = s[:s.shape[0] - step] + s[step:]
            step *= 2
        win = s[POOL_HALO - half:POOL_HALO - half + tm]
        count = jnp.minimum(row + half, seq_len) - jnp.maximum(row - half, 0)
        pooled = win * (1.0 / count.astype(F32)) - p
        y_b = jnp.dot(pooled.astype(BF16), wp_ref[g], preferred_element_type=F32) * ps_ref[:, cols]
        put(slice(None), D_A + g * D_POOL_GROUP, y_b)


def _mixer(u, v, p, w_spatial, b_spatial, w_pool, pool_scale, *, seq_len, panel, tm=512):
    t = u.shape[0]
    assert seq_len % tm == 0 and tm % CHUNK == 0 and panel % HEAD_DIM == 0 and panel % D_POOL_GROUP == 0
    halo_blocks_per_tile = tm // POOL_HALO
    n_halo_blocks = t // POOL_HALO
    row_spec = lambda width: pl.BlockSpec((tm, width), lambda i: (i, 0))
    const3 = lambda shape: pl.BlockSpec(shape, lambda i: (0, 0, 0))
    blocks = [((tm, D_A), BF16)] * 2 + [((tm, D_B), F32), ((tm, D_MODEL), BF16),
              (w_spatial.shape, BF16), (b_spatial.shape, F32), (w_pool.shape, BF16)]
    return pl.pallas_call(
        functools.partial(_mixer_kernel, tiles_per_seq=seq_len // tm, seq_len=seq_len),
        grid=(t // tm,),
        in_specs=[
            row_spec(D_A), row_spec(D_A), row_spec(D_B),
            pl.BlockSpec((POOL_HALO, D_B),
                         lambda i: (jnp.maximum(i * halo_blocks_per_tile - 1, 0), 0)),
            pl.BlockSpec((POOL_HALO, D_B),
                         lambda i: (jnp.minimum((i + 1) * halo_blocks_per_tile, n_halo_blocks - 1), 0)),
            const3(w_spatial.shape), const3(b_spatial.shape), const3(w_pool.shape),
            pl.BlockSpec((1, D_B), lambda i: (0, 0)),
        ],
        out_specs=pl.BlockSpec((D_MODEL // panel, tm, panel), lambda i: (0, i, 0)),
        out_shape=jax.ShapeDtypeStruct((D_MODEL // panel, t, panel), BF16),
        compiler_params=_params(("parallel",),
                                _vmem_limit(blocks, [((tm + 2 * POOL_HALO, D_B), F32)] * 3)),
        name="token_mixer",
    )(u, v, p, p, p, w_spatial, b_spatial, w_pool, pool_scale)


W_OUT_TK = 512
_FFN_UP = functools.partial(_ffn_up, tm=1024, tn=512)
_FFN_DOWN = functools.partial(_down_residual, tm=1024, tk=768, chunk=512)
_W_OUT = functools.partial(_down_residual, tm=1024, tk=W_OUT_TK, chunk=512)
_W_IN = functools.partial(_w_in, tm=1024, tn=512)


def _as_rows(x):
    b, s, d = x.shape
    return x.reshape(b * s, d)


def _mix_out(u, v, p, h1, w, seq_len):
    y_cat = _mixer(u, v, p, w["w_spatial"], w["b_spatial"], w["w_pool"], w["pool_scale"],
                   seq_len=seq_len, panel=W_OUT_TK)
    return _W_OUT(y_cat, None, w["w_out"], h1, w["mix_post"], w["ffn2_pre"],
                  scale=1.0, emit_residual=True, name="w_out")


def _ffn2(h2, xn, w, out_shape):
    act, act_tail, _ = _FFN_UP(xn, w["ffn2_gate"], w["ffn2_up"])
    (out,) = _FFN_DOWN(act, act_tail, w["ffn2_down"], h2, w["ffn2_post"], w["final_norm"],
                       scale=FFN_RESIDUAL, emit_residual=False, name="ffn2_down")
    return out.reshape(out_shape)


def kernel(x_prompt, x_sample, ffn1_pre, ffn1_post, ffn1_gate, ffn1_up, ffn1_down, mix_pre, mix_post, w_in, w_spatial, b_spatial, v_norm, w_pool, pool_scale, w_out, ffn2_pre, ffn2_post, ffn2_gate, ffn2_up, ffn2_down, final_norm):
    assert ffn1_pre.shape[0] == 1, "single-layer trunk"
    row = lambda g: g[0].reshape(1, -1).astype(F32)
    mxu = lambda m: m[0].astype(BF16)
    raw = lambda m: m[0].astype(F32)
    w = {
        "ffn1_pre": row(ffn1_pre), "ffn1_post": row(ffn1_post),
        "ffn1_gate": mxu(ffn1_gate), "ffn1_up": mxu(ffn1_up),
        "mix_pre": row(mix_pre), "mix_post": row(mix_post),
        "w_spatial": mxu(w_spatial),
        "b_spatial": jnp.broadcast_to(b_spatial[0][:, :, None], (N_HEADS, CHUNK, HEAD_DIM)).astype(F32),
        "v_norm": row(v_norm), "w_pool": mxu(w_pool), "pool_scale": row(pool_scale),
        "ffn2_pre": row(ffn2_pre), "ffn2_post": row(ffn2_post),
        "final_norm": row(final_norm),
    }
    h0_p, h0_s = _as_rows(x_prompt), _as_rows(x_sample)
    act_p, tail_p, (w["ffn1_down"],) = _FFN_UP(
        _rmsnorm_rows(h0_p, w["ffn1_pre"]), w["ffn1_gate"], w["ffn1_up"], (raw(ffn1_down),),
        side_rows=FFN1_DOWN_CAST_ROWS)
    act_s, tail_s, (w["w_in"],) = _FFN_UP(
        _rmsnorm_rows(h0_s, w["ffn1_pre"]), w["ffn1_gate"], w["ffn1_up"], (raw(w_in),),
        side_rows=W_IN_CAST_ROWS)
    ffn1_down_args = dict(scale=FFN_RESIDUAL, emit_residual=True, name="ffn1_down")
    h1_p, a_p = _FFN_DOWN(act_p, tail_p, w["ffn1_down"], h0_p, w["ffn1_post"], w["mix_pre"], **ffn1_down_args)
    h1_s, a_s = _FFN_DOWN(act_s, tail_s, w["ffn1_down"], h0_s, w["ffn1_post"], w["mix_pre"], **ffn1_down_args)

    u_s, v_s, p_s, (w["ffn2_down"], w["w_out"]) = _W_IN(
        a_s, w["w_in"], w["v_norm"], (raw(ffn2_down), raw(w_out)), side_rows=SQUARE_CAST_ROWS)
    u_p, v_p, p_p, (w["ffn2_gate"], w["ffn2_up"]) = _W_IN(
        a_p, w["w_in"], w["v_norm"], (raw(ffn2_gate), raw(ffn2_up)), side_rows=FFN_UP_CAST_ROWS)
    h2_p, xn_p = _mix_out(u_p, v_p, p_p, h1_p, w, x_prompt.shape[1])
    h2_s, xn_s = _mix_out(u_s, v_s, p_s, h1_s, w, x_sample.shape[1])
    return _ffn2(h2_p, xn_p, w, x_prompt.shape), _ffn2(h2_s, xn_s, w, x_sample.shape)
```

```python
import functools
import math

import jax
import jax.numpy as jnp
from jax import lax
from jax.experimental import pallas as pl
from jax.experimental.pallas import tpu as pltpu

F32 = jnp.float32
BF16 = jnp.bfloat16

D_MODEL = 4096
D_A = D_MODEL // 2
HEAD_DIM = 128
N_HEADS = D_A // HEAD_DIM
CHUNK = 128
D_B = D_MODEL - D_A
POOL_WINDOWS = (2, 4, 8, 16)
D_POOL_GROUP = D_B // len(POOL_WINDOWS)
POOL_HALO = max(POOL_WINDOWS) // 2
EPS = 1e-6
FFN_RESIDUAL = 0.5

V7X_LANES = 128
V7X_VMEM_BYTES = 64 * 1024 * 1024
COMPILER_SCRATCH_BYTES = 4 * 1024 * 1024
PANEL_DMA_PRIORITY = 1


def _nbytes(shape, dtype):
    return math.prod(shape) * jnp.dtype(dtype).itemsize


def _vmem_limit(pipelined_blocks, temporaries=()):
    need = 2 * sum(_nbytes(s, d) for s, d in pipelined_blocks)
    need += sum(_nbytes(s, d) for s, d in temporaries)
    need += COMPILER_SCRATCH_BYTES
    assert need <= V7X_VMEM_BYTES, need
    return need


def _params(semantics, vmem_bytes):
    return pltpu.CompilerParams(dimension_semantics=semantics, vmem_limit_bytes=vmem_bytes)


def _rms_scale(x):
    return lax.rsqrt(jnp.mean(x * x, axis=-1, keepdims=True) + EPS)


def _gelu(x):
    return 0.5 * x * (1.0 + lax.erf(x * math.sqrt(0.5)))


def _lane_fold(x):
    acc = x[:, :V7X_LANES]
    for c in range(1, x.shape[1] // V7X_LANES):
        acc = acc + x[:, c * V7X_LANES:(c + 1) * V7X_LANES]
    return acc


def _split_tail(n, tile):
    n_tail = n % tile
    n_main = n - n_tail
    assert n_tail % V7X_LANES == 0 and (n_tail == 0 or n_main % n_tail == 0)
    return n_main, n_tail


def _norm_kernel(x_ref, g_ref, o_ref):
    x = x_ref[...]
    o_ref[...] = (x * _rms_scale(x) * g_ref[...]).astype(o_ref.dtype)


def _rmsnorm_rows(x, gain, *, rows=512):
    t, d = x.shape
    blocks = [((rows, d), F32), ((rows, d), BF16)]
    return pl.pallas_call(
        _norm_kernel,
        grid=(t // rows,),
        in_specs=[pl.BlockSpec((rows, d), lambda i: (i, 0)),
                  pl.BlockSpec((1, d), lambda i: (0, 0))],
        out_specs=pl.BlockSpec((rows, d), lambda i: (i, 0)),
        out_shape=jax.ShapeDtypeStruct((t, d), BF16),
        compiler_params=_params(("parallel",), _vmem_limit(blocks, [((rows, d), F32)] * 2)),
        name="rmsnorm_rows",
    )(x, gain)


FFN1_DOWN_CAST_ROWS = 64
W_IN_CAST_ROWS = 32
FFN_UP_CAST_ROWS = 64
SQUARE_CAST_ROWS = 256


def _side_cast_step(step, srcs, dsts, f32_stage, bf16_stage, sems):
    rows = f32_stage.shape[1]
    starts = [0]
    for src in srcs:
        starts.append(starts[-1] + src.shape[0] // rows)
    n_blocks = starts[-1]

    def per_array(blk, fn):
        for a, (src, dst) in enumerate(zip(srcs, dsts)):
            @pl.when(jnp.logical_and(blk >= starts[a], blk < starts[a + 1]))
            def _():
                fn(src, dst, pl.ds((blk - starts[a]) * rows, rows))

    def load(src, block_rows, blk):
        return pltpu.make_async_copy(src.at[block_rows, :], f32_stage.at[blk % 2], sems.at[blk % 2])

    def store(dst, block_rows, blk):
        return pltpu.make_async_copy(bf16_stage.at[blk % 2], dst.at[block_rows, :], sems.at[2 + blk % 2])

    first_rows = pl.ds(0, rows)

    @pl.when(step == 0)
    def _():
        load(srcs[0], first_rows, step).start(priority=PANEL_DMA_PRIORITY)

    @pl.when(step + 1 < n_blocks)
    def _():
        per_array(step + 1, lambda src, dst, r: load(src, r, step + 1).start(priority=PANEL_DMA_PRIORITY))

    @pl.when(step < n_blocks)
    def _():
        load(srcs[0], first_rows, step).wait()

        @pl.when(step >= 2)
        def _():
            store(dsts[0], first_rows, step - 2).wait()
        bf16_stage[step % 2] = f32_stage[step % 2].astype(bf16_stage.dtype)
        per_array(step, lambda src, dst, r: store(dst, r, step).start(priority=PANEL_DMA_PRIORITY))

    @pl.when(step == n_blocks - 1)
    def _():
        if n_blocks > 1:
            store(dsts[0], first_rows, step - 1).wait()
        store(dsts[0], first_rows, step).wait()


def _side_cast_plan(side, side_rows, n_steps):
    if not side:
        return [], [], []
    cols = side[0].shape[1]
    assert all(s.shape[1] == cols and s.dtype == F32 and s.shape[0] % side_rows == 0 for s in side)
    assert sum(s.shape[0] // side_rows for s in side) <= n_steps
    any_specs = [pl.BlockSpec(memory_space=pl.ANY)] * len(side)
    out_shape = [jax.ShapeDtypeStruct(s.shape, BF16) for s in side]
    scratch = [((2, side_rows, cols), F32), ((2, side_rows, cols), BF16)]
    return any_specs, out_shape, scratch


def _scratch_shapes(side_scratch):
    if not side_scratch:
        return []
    return [pltpu.VMEM(s, d) for s, d in side_scratch] + [pltpu.SemaphoreType.DMA((4,))]


def _swiglu_tile(x, wg_ref, wu_ref):
    g = jnp.dot(x, wg_ref[...], preferred_element_type=F32)
    u = jnp.dot(x, wu_ref[...], preferred_element_type=F32)
    return (g * jax.nn.sigmoid(g) * u).astype(BF16)


def _ffn_up_kernel(x_ref, wg_ref, wu_ref, *rest, has_tail, n_side):
    rest = list(rest)
    wgt_ref, wut_ref = (rest.pop(0), rest.pop(0)) if has_tail else (None, None)
    side_srcs = [rest.pop(0) for _ in range(n_side)]
    o_ref = rest.pop(0)
    ot_ref = rest.pop(0) if has_tail else None
    side_dsts = [rest.pop(0) for _ in range(n_side)]
    if n_side:
        step = pl.program_id(0) * pl.num_programs(1) + pl.program_id(1)
        _side_cast_step(step, side_srcs, side_dsts, *rest)
    o_ref[...] = _swiglu_tile(x_ref[...], wg_ref, wu_ref)
    if has_tail:
        @pl.when(pl.program_id(1) == 0)
        def _():
            ot_ref[...] = _swiglu_tile(x_ref[...], wgt_ref, wut_ref)


def _ffn_up(x, w_gate, w_up, side=(), *, tm, tn, side_rows=None):
    t, k = x.shape
    n_main, n_tail = _split_tail(w_gate.shape[1], tn)
    assert t % tm == 0
    grid = (t // tm, n_main // tn)
    side_specs, side_out, side_scratch = _side_cast_plan(side, side_rows, grid[0] * grid[1])
    w_spec = pl.BlockSpec((k, tn), lambda i, j: (0, j))
    in_specs = [pl.BlockSpec((tm, k), lambda i, j: (i, 0)), w_spec, w_spec]
    out_specs = [pl.BlockSpec((tm, tn), lambda i, j: (i, j))]
    out_shape = [jax.ShapeDtypeStruct((t, n_main), BF16)]
    operands = [x, w_gate, w_up]
    blocks = [((tm, k), x.dtype), ((k, tn), BF16), ((k, tn), BF16), ((tm, tn), BF16)]
    if n_tail:
        tail_spec = pl.BlockSpec((k, n_tail), lambda i, j: (0, n_main // n_tail),
                                 pipeline_mode=pl.Buffered(1))
        in_specs += [tail_spec, tail_spec]
        out_specs.append(pl.BlockSpec((tm, n_tail), lambda i, j: (i, 0)))
        out_shape.append(jax.ShapeDtypeStruct((t, n_tail), BF16))
        operands += [w_gate, w_up]
        blocks += [((k, n_tail), BF16), ((tm, n_tail), BF16)]
    outs = pl.pallas_call(
        functools.partial(_ffn_up_kernel, has_tail=bool(n_tail), n_side=len(side)),
        grid=grid,
        in_specs=in_specs + side_specs,
        out_specs=out_specs + side_specs,
        out_shape=out_shape + side_out,
        scratch_shapes=_scratch_shapes(side_scratch),
        compiler_params=_params(("arbitrary", "arbitrary"),
                                _vmem_limit(blocks, side_scratch + [((tm, tn), F32)] * 4)),
        name="ffn_up",
    )(*operands, *side)
    n_out = 2 if n_tail else 1
    return outs[0], (outs[1] if n_tail else None), list(outs[n_out:])


def _w_in_kernel(x_ref, w_ref, vg_ref, *rest, n_u, n_v, n_side):
    side_srcs, rest = rest[:n_side], rest[n_side:]
    (u_ref, v_ref, p_ref), rest = rest[:3], rest[3:]
    side_dsts, scratch = rest[:n_side], rest[n_side:]
    j = pl.program_id(1)
    if n_side:
        _side_cast_step(pl.program_id(0) * pl.num_programs(1) + j, side_srcs, side_dsts, *scratch)

    matmul = lambda: jnp.dot(x_ref[...], w_ref[...], preferred_element_type=F32)

    @pl.when(j < n_u)
    def _():
        u_ref[...] = _gelu(matmul()).astype(u_ref.dtype)

    @pl.when(jnp.logical_and(j >= n_u, j < n_u + n_v))
    def _():
        g = _gelu(matmul())
        for hd in range(v_ref.shape[1] // HEAD_DIM):
            cols = slice(hd * HEAD_DIM, (hd + 1) * HEAD_DIM)
            gh = g[:, cols]
            v_ref[:, cols] = (gh * _rms_scale(gh) * vg_ref[:, cols]).astype(v_ref.dtype)

    @pl.when(j >= n_u + n_v)
    def _():
        p_ref[...] = matmul()


def _w_in(x, w, v_gain, side=(), *, tm, tn, side_rows=None):
    t, k = x.shape
    assert t % tm == 0 and D_A % tn == 0 and D_B % tn == 0 and w.shape[1] == 2 * D_A + D_B
    n_u = n_v = D_A // tn
    n_p = D_B // tn
    grid = (t // tm, n_u + n_v + n_p)
    clamp = lambda j, lo, cnt: jnp.clip(j - lo, 0, cnt - 1)
    side_specs, side_out, side_scratch = _side_cast_plan(side, side_rows, grid[0] * grid[1])
    blocks = [((tm, k), x.dtype), ((k, tn), BF16), ((tm, tn), BF16), ((tm, tn), BF16), ((tm, tn), F32)]
    outs = pl.pallas_call(
        functools.partial(_w_in_kernel, n_u=n_u, n_v=n_v, n_side=len(side)),
        grid=grid,
        in_specs=[pl.BlockSpec((tm, k), lambda i, j: (i, 0)),
                  pl.BlockSpec((k, tn), lambda i, j: (0, j)),
                  pl.BlockSpec((1, tn), lambda i, j: (0, clamp(j, n_u, n_v)))] + side_specs,
        out_specs=[pl.BlockSpec((tm, tn), lambda i, j: (i, clamp(j, 0, n_u))),
                   pl.BlockSpec((tm, tn), lambda i, j: (i, clamp(j, n_u, n_v))),
                   pl.BlockSpec((tm, tn), lambda i, j: (i, clamp(j, n_u + n_v, n_p)))] + side_specs,
        out_shape=[jax.ShapeDtypeStruct((t, D_A), BF16), jax.ShapeDtypeStruct((t, D_A), BF16),
                   jax.ShapeDtypeStruct((t, D_B), F32)] + side_out,
        scratch_shapes=_scratch_shapes(side_scratch),
        compiler_params=_params(("arbitrary", "arbitrary"),
                                _vmem_limit(blocks, side_scratch + [((tm, tn), F32)] * 3)),
        name="w_in",
    )(x, w, v_gain, *side)
    return outs[0], outs[1], outs[2], list(outs[3:])


EPILOGUE_ROWS = 128
STAGE_SLOTS = 4


def _down_kernel(*refs, scale, emit_residual, has_tail, chunk):
    refs = list(refs)
    xm_ref = refs.pop(0)
    xt_ref = refs.pop(0) if has_tail else None
    wm_ref = refs.pop(0)
    wt_ref = refs.pop(0) if has_tail else None
    gp_ref, gn_ref, h_hbm = refs.pop(0), refs.pop(0), refs.pop(0)
    hout_hbm = refs.pop(0) if emit_residual else None
    nxt_hbm = refs.pop(0)
    y_scr, ss_scr, h_scr = refs.pop(0), refs.pop(0), refs.pop(0)
    stage = refs.pop(0) if emit_residual else None
    (sems,) = refs

    i, k = pl.program_id(0), pl.program_id(1)
    n_i, n_k = pl.num_programs(0), pl.num_programs(1)
    tm, n_cols = y_scr.shape
    col_chunks = [slice(c * chunk, (c + 1) * chunk) for c in range(n_cols // chunk)]
    n_slots = stage.shape[0] if emit_residual else 0
    rows = lambda p: pl.ds(p * tm, tm)

    def fetch(p):
        return pltpu.make_async_copy(h_hbm.at[rows(p), :], h_scr, sems.at[0])

    def store_panel(p):
        dst = hout_hbm if emit_residual else nxt_hbm
        return pltpu.make_async_copy(h_scr, dst.at[rows(p), :], sems.at[1])

    rb = stage.shape[1] if emit_residual else EPILOGUE_ROWS
    row_blocks = [slice(b * rb, (b + 1) * rb) for b in range(tm // rb)]

    def store_stage(p, b):
        slot = b % n_slots
        return pltpu.make_async_copy(stage.at[slot], nxt_hbm.at[pl.ds(p * tm + b * rb, rb), :],
                                     sems.at[2 + slot])

    def outstanding(p):
        last = range(len(row_blocks) - n_slots, len(row_blocks))
        return [store_panel(p)] + [store_stage(p, b) for b in last]

    def partial_y(cols):
        return jnp.dot(xm_ref[...], wm_ref[:, cols], preferred_element_type=F32)

    def first_y(cols):
        y = partial_y(cols)
        if has_tail:
            y = y + jnp.dot(xt_ref[...], wt_ref[:, cols], preferred_element_type=F32)
        return y

    def row_scale(ss):
        return lax.rsqrt(jnp.sum(ss, axis=-1, keepdims=True) * (1.0 / n_cols) + EPS)

    def epilogue_rows(p, b):
        rws = row_blocks[b]
        r1 = scale * row_scale(ss_scr[rws, :])
        sq = jnp.zeros((rb, V7X_LANES), F32)
        for cols in col_chunks:
            h_new = h_scr[rws, cols] + y_scr[rws, cols] * r1 * gp_ref[:, cols]
            h_scr[rws, cols] = h_new
            sq = sq + _lane_fold(h_new * h_new)
        r2 = row_scale(sq)
        if emit_residual and b >= n_slots:
            store_stage(p, b - n_slots).wait()
        for cols in col_chunks:
            nxt = h_scr[rws, cols] * r2 * gn_ref[:, cols]
            if emit_residual:
                stage[b % n_slots, :, cols] = nxt.astype(stage.dtype)
            else:
                h_scr[rws, cols] = nxt
        if emit_residual:
            store_stage(p, b).start(priority=PANEL_DMA_PRIORITY)

    has_prev = i > 0

    @pl.when(k == 0)
    def _():
        for cols in col_chunks:
            y_scr[:, cols] = first_y(cols)

    @pl.when(jnp.logical_and(k > 0, k < n_k - 1))
    def _():
        for cols in col_chunks:
            y_scr[:, cols] += partial_y(cols)

    @pl.when(k == n_k // 2)
    def _():
        @pl.when(has_prev)
        def _():
            for copy in outstanding(i - 1):
                copy.wait()
        fetch(i).start(priority=PANEL_DMA_PRIORITY)

    @pl.when(k == n_k - 1)
    def _():
        sq = jnp.zeros((tm, V7X_LANES), F32)
        for cols in col_chunks:
            y = y_scr[:, cols] + partial_y(cols)
            y_scr[:, cols] = y
            sq = sq + _lane_fold(y * y)
        ss_scr[...] = sq

    @pl.when(k == n_k - 1)
    def _():
        fetch(i).wait()
        for b in range(len(row_blocks)):
            epilogue_rows(i, b)
        store_panel(i).start(priority=PANEL_DMA_PRIORITY)

        @pl.when(i == n_i - 1)
        def _():
            for copy in outstanding(i):
                copy.wait()


def _down_residual(x_main, x_tail, w, h, gain_post, gain_next, *, scale, emit_residual,
                   tm, tk, chunk, name):
    if x_main.ndim == 3:
        n_k, t, _ = x_main.shape
        assert x_main.shape[2] == tk
        k_main = n_k * tk
        x_spec = pl.BlockSpec((None, tm, tk), lambda i, k: (k, i, 0))
    else:
        t, k_main = x_main.shape
        n_k = k_main // tk
        x_spec = pl.BlockSpec((tm, tk), lambda i, k: (i, k))
    k_tail = 0 if x_tail is None else x_tail.shape[1]
    n = w.shape[1]
    assert t % tm == 0 and tm % EPILOGUE_ROWS == 0 and k_main % tk == 0 and n % chunk == 0
    assert n_k >= 4 and h.shape == (t, n) and w.shape[0] == k_main + k_tail
    assert k_tail == 0 or k_main % k_tail == 0

    any_spec = pl.BlockSpec(memory_space=pl.ANY)
    once = dict(pipeline_mode=pl.Buffered(1))
    gain_spec = pl.BlockSpec((1, n), lambda i, k: (0, 0), **once)
    in_specs = [x_spec]
    operands = [x_main]
    blocks = [((tm, tk), BF16), ((tk, n), BF16)]
    if k_tail:
        in_specs.append(pl.BlockSpec((tm, k_tail), lambda i, k: (i, 0)))
        operands.append(x_tail)
        blocks.append(((tm, k_tail), BF16))
    in_specs.append(pl.BlockSpec((tk, n), lambda i, k: (k, 0)))
    operands.append(w)
    if k_tail:
        in_specs.append(pl.BlockSpec((k_tail, n), lambda i, k: (k_main // k_tail, 0), **once))
        operands.append(w)
    in_specs += [gain_spec, gain_spec, any_spec]
    operands += [gain_post, gain_next, h]

    out_shape = [jax.ShapeDtypeStruct((t, n), BF16 if emit_residual else F32)]
    scratch = [((tm, n), F32), ((tm, V7X_LANES), F32), ((tm, n), F32)]
    n_sems = 2
    if emit_residual:
        out_shape.insert(0, jax.ShapeDtypeStruct((t, n), F32))
        n_slots = min(STAGE_SLOTS, tm // EPILOGUE_ROWS)
        scratch.append(((n_slots, EPILOGUE_ROWS, n), BF16))
        n_sems = 2 + n_slots
    single = scratch + [((k_tail, n), BF16), ((2, n), F32)]
    return pl.pallas_call(
        functools.partial(_down_kernel, scale=scale, emit_residual=emit_residual,
                          has_tail=bool(k_tail), chunk=chunk),
        grid=(t // tm, n_k),
        in_specs=in_specs,
        out_specs=[any_spec] * len(out_shape),
        out_shape=out_shape,
        scratch_shapes=[pltpu.VMEM(s, d) for s, d in scratch] + [pltpu.SemaphoreType.DMA((n_sems,))],
        compiler_params=_params(("arbitrary", "arbitrary"), _vmem_limit(blocks, single)),
        name=name,
    )(*operands)


def _mixer_kernel(u_ref, v_ref, p_ref, pprev_ref, pnext_ref, ws_ref, bs_ref, wp_ref, ps_ref,
                  o_ref, *, tiles_per_seq, seq_len):
    tm = u_ref.shape[0]
    n_chunks = tm // CHUNK
    panel = o_ref.shape[2]

    def put(rows, col, value):
        off = col % panel
        o_ref[col // panel, rows, off:off + value.shape[1]] = value.astype(o_ref.dtype)

    for hd in range(N_HEADS):
        cols = slice(hd * HEAD_DIM, (hd + 1) * HEAD_DIM)
        v_h = jnp.concatenate(
            [v_ref[c * CHUNK:(c + 1) * CHUNK, cols] for c in range(n_chunks)], axis=1)
        sv = jnp.dot(ws_ref[hd], v_h, preferred_element_type=F32)
        bias = bs_ref[hd]
        for c in range(n_chunks):
            rows = slice(c * CHUNK, (c + 1) * CHUNK)
            gate = sv[:, c * HEAD_DIM:(c + 1) * HEAD_DIM] + bias
            put(rows, hd * HEAD_DIM, u_ref[rows, cols].astype(F32) * gate)

    pos = pl.program_id(0) % tiles_per_seq
    prev = jnp.where(pos > 0, pprev_ref[...], 0.0)
    nxt = jnp.where(pos < tiles_per_seq - 1, pnext_ref[...], 0.0)
    row = lax.broadcasted_iota(jnp.int32, (tm, 1), 0) + pos * tm
    for g, w in enumerate(POOL_WINDOWS):
        half = w // 2
        cols = slice(g * D_POOL_GROUP, (g + 1) * D_POOL_GROUP)
        p = p_ref[:, cols]
        s = jnp.concatenate([prev[:, cols], p, nxt[:, cols]], axis=0)
        step = 1
        while step < w:
            s = s[:s.shape[0] - step] + s[step:]
            step *= 2
        win = s[POOL_HALO - half:POOL_HALO - half + tm]
        count = jnp.minimum(row + half, seq_len) - jnp.maximum(row - half, 0)
        pooled = win * (1.0 / count.astype(F32)) - p
        y_b = jnp.dot(pooled.astype(BF16), wp_ref[g], preferred_element_type=F32) * ps_ref[:, cols]
        put(slice(None), D_A + g * D_POOL_GROUP, y_b)


def _mixer(u, v, p, w_spatial, b_spatial, w_pool, pool_scale, *, seq_len, panel, tm=512):
    t = u.shape[0]
    assert seq_len % tm == 0 and tm % CHUNK == 0 and panel % HEAD_DIM == 0 and panel % D_POOL_GROUP == 0
    halo_blocks_per_tile = tm // POOL_HALO
    n_halo_blocks = t // POOL_HALO
    row_spec = lambda width: pl.BlockSpec((tm, width), lambda i: (i, 0))
    const3 = lambda shape: pl.BlockSpec(shape, lambda i: (0, 0, 0))
    blocks = [((tm, D_A), BF16)] * 2 + [((tm, D_B), F32), ((tm, D_MODEL), BF16),
              (w_spatial.shape, BF16), (b_spatial.shape, F32), (w_pool.shape, BF16)]
    return pl.pallas_call(
        functools.partial(_mixer_kernel, tiles_per_seq=seq_len // tm, seq_len=seq_len),
        grid=(t // tm,),
        in_specs=[
            row_spec(D_A), row_spec(D_A), row_spec(D_B),
            pl.BlockSpec((POOL_HALO, D_B),
                         lambda i: (jnp.maximum(i * halo_blocks_per_tile - 1, 0), 0)),
            pl.BlockSpec((POOL_HALO, D_B),
                         lambda i: (jnp.minimum((i + 1) * halo_blocks_per_tile, n_halo_blocks - 1), 0)),
            const3(w_spatial.shape), const3(b_spatial.shape), const3(w_pool.shape),
            pl.BlockSpec((1, D_B), lambda i: (0, 0)),
        ],
        out_specs=pl.BlockSpec((D_MODEL // panel, tm, panel), lambda i: (0, i, 0)),
        out_shape=jax.ShapeDtypeStruct((D_MODEL // panel, t, panel), BF16),
        compiler_params=_params(("parallel",),
                                _vmem_limit(blocks, [((tm + 2 * POOL_HALO, D_B), F32)] * 3)),
        name="token_mixer",
    )(u, v, p, p, p, w_spatial, b_spatial, w_pool, pool_scale)


W_OUT_TK = 1024
_FFN_UP = functools.partial(_ffn_up, tm=1024, tn=512)
_FFN_DOWN = functools.partial(_down_residual, tm=1024, tk=768, chunk=512)
_W_OUT = functools.partial(_down_residual, tm=1024, tk=W_OUT_TK, chunk=512)
_W_IN = functools.partial(_w_in, tm=1024, tn=512)


def _as_rows(x):
    b, s, d = x.shape
    return x.reshape(b * s, d)


def _mix_out(u, v, p, h1, w, seq_len):
    y_cat = _mixer(u, v, p, w["w_spatial"], w["b_spatial"], w["w_pool"], w["pool_scale"],
                   seq_len=seq_len, panel=W_OUT_TK)
    return _W_OUT(y_cat, None, w["w_out"], h1, w["mix_post"], w["ffn2_pre"],
                  scale=1.0, emit_residual=True, name="w_out")


def _ffn2(h2, xn, w, out_shape):
    act, act_tail, _ = _FFN_UP(xn, w["ffn2_gate"], w["ffn2_up"])
    (out,) = _FFN_DOWN(act, act_tail, w["ffn2_down"], h2, w["ffn2_post"], w["final_norm"],
                       scale=FFN_RESIDUAL, emit_residual=False, name="ffn2_down")
    return out.reshape(out_shape)


def kernel(x_prompt, x_sample, ffn1_pre, ffn1_post, ffn1_gate, ffn1_up, ffn1_down, mix_pre, mix_post, w_in, w_spatial, b_spatial, v_norm, w_pool, pool_scale, w_out, ffn2_pre, ffn2_post, ffn2_gate, ffn2_up, ffn2_down, final_norm):
    assert ffn1_pre.shape[0] == 1, "single-layer trunk"
    row = lambda g: g[0].reshape(1, -1).astype(F32)
    mxu = lambda m: m[0].astype(BF16)
    raw = lambda m: m[0].astype(F32)
    w = {
        "ffn1_pre": row(ffn1_pre), "ffn1_post": row(ffn1_post),
        "ffn1_gate": mxu(ffn1_gate), "ffn1_up": mxu(ffn1_up),
        "mix_pre": row(mix_pre), "mix_post": row(mix_post),
        "w_spatial": mxu(w_spatial),
        "b_spatial": jnp.broadcast_to(b_spatial[0][:, :, None], (N_HEADS, CHUNK, HEAD_DIM)).astype(F32),
        "v_norm": row(v_norm), "w_pool": mxu(w_pool), "pool_scale": row(pool_scale),
        "ffn2_pre": row(ffn2_pre), "ffn2_post": row(ffn2_post),
        "final_norm": row(final_norm),
    }
    h0_p, h0_s = _as_rows(x_prompt), _as_rows(x_sample)
    act_p, tail_p, (w["ffn1_down"],) = _FFN_UP(
        _rmsnorm_rows(h0_p, w["ffn1_pre"]), w["ffn1_gate"], w["ffn1_up"], (raw(ffn1_down),),
        side_rows=FFN1_DOWN_CAST_ROWS)
    act_s, tail_s, (w["w_in"],) = _FFN_UP(
        _rmsnorm_rows(h0_s, w["ffn1_pre"]), w["ffn1_gate"], w["ffn1_up"], (raw(w_in),),
        side_rows=W_IN_CAST_ROWS)
    ffn1_down_args = dict(scale=FFN_RESIDUAL, emit_residual=True, name="ffn1_down")
    h1_p, a_p = _FFN_DOWN(act_p, tail_p, w["ffn1_down"], h0_p, w["ffn1_post"], w["mix_pre"], **ffn1_down_args)
    h1_s, a_s = _FFN_DOWN(act_s, tail_s, w["ffn1_down"], h0_s, w["ffn1_post"], w["mix_pre"], **ffn1_down_args)

    u_s, v_s, p_s, (w["ffn2_down"], w["w_out"]) = _W_IN(
        a_s, w["w_in"], w["v_norm"], (raw(ffn2_down), raw(w_out)), side_rows=SQUARE_CAST_ROWS)
    u_p, v_p, p_p, (w["ffn2_gate"], w["ffn2_up"]) = _W_IN(
        a_p, w["w_in"], w["v_norm"], (raw(ffn2_gate), raw(ffn2_up)), side_rows=FFN_UP_CAST_ROWS)
    h2_p, xn_p = _mix_out(u_p, v_p, p_p, h1_p, w, x_prompt.shape[1])
    h2_s, xn_s = _mix_out(u_s, v_s, p_s, h1_s, w, x_sample.shape[1])
    return _ffn2(h2_p, xn_p, w, x_prompt.shape), _ffn2(h2_s, xn_s, w, x_sample.shape)
```

```python
import functools
import math

import jax
import jax.numpy as jnp
from jax import lax
from jax.experimental import pallas as pl
from jax.experimental.pallas import tpu as pltpu

F32 = jnp.float32
BF16 = jnp.bfloat16

D_MODEL = 4096
D_A = D_MODEL // 2
HEAD_DIM = 128
N_HEADS = D_A // HEAD_DIM
CHUNK = 128
D_B = D_MODEL - D_A
POOL_WINDOWS = (2, 4, 8, 16)
D_POOL_GROUP = D_B // len(POOL_WINDOWS)
POOL_HALO = max(POOL_WINDOWS) // 2
EPS = 1e-6
FFN_RESIDUAL = 0.5

V7X_LANES = 128
V7X_VMEM_BYTES = 64 * 1024 * 1024
COMPILER_SCRATCH_BYTES = 4 * 1024 * 1024
PANEL_DMA_PRIORITY = 1


def _nbytes(shape, dtype):
    return math.prod(shape) * jnp.dtype(dtype).itemsize


def _vmem_limit(pipelined_blocks, temporaries=()):
    need = 2 * sum(_nbytes(s, d) for s, d in pipelined_blocks)
    need += sum(_nbytes(s, d) for s, d in temporaries)
    need += COMPILER_SCRATCH_BYTES
    assert need <= V7X_VMEM_BYTES, need
    return need


def _params(semantics, vmem_bytes):
    return pltpu.CompilerParams(dimension_semantics=semantics, vmem_limit_bytes=vmem_bytes)


def _rms_scale(x):
    return lax.rsqrt(jnp.mean(x * x, axis=-1, keepdims=True) + EPS)


def _gelu(x):
    return 0.5 * x * (1.0 + lax.erf(x * math.sqrt(0.5)))


def _lane_fold(x):
    acc = x[:, :V7X_LANES]
    for c in range(1, x.shape[1] // V7X_LANES):
        acc = acc + x[:, c * V7X_LANES:(c + 1) * V7X_LANES]
    return acc


def _split_tail(n, tile):
    n_tail = n % tile
    n_main = n - n_tail
    assert n_tail % V7X_LANES == 0 and (n_tail == 0 or n_main % n_tail == 0)
    return n_main, n_tail


def _norm_kernel(x_ref, g_ref, o_ref):
    x = x_ref[...]
    o_ref[...] = (x * _rms_scale(x) * g_ref[...]).astype(o_ref.dtype)


def _rmsnorm_rows(x, gain, *, rows=512):
    t, d = x.shape
    blocks = [((rows, d), F32), ((rows, d), BF16)]
    return pl.pallas_call(
        _norm_kernel,
        grid=(t // rows,),
        in_specs=[pl.BlockSpec((rows, d), lambda i: (i, 0)),
                  pl.BlockSpec((1, d), lambda i: (0, 0))],
        out_specs=pl.BlockSpec((rows, d), lambda i: (i, 0)),
        out_shape=jax.ShapeDtypeStruct((t, d), BF16),
        compiler_params=_params(("parallel",), _vmem_limit(blocks, [((rows, d), F32)] * 2)),
        name="rmsnorm_rows",
    )(x, gain)


FFN1_DOWN_CAST_ROWS = 64
W_IN_CAST_ROWS = 32
FFN_UP_CAST_ROWS = 64
SQUARE_CAST_ROWS = 256


def _side_cast_step(step, srcs, dsts, f32_stage, bf16_stage, sems):
    rows = f32_stage.shape[1]
    starts = [0]
    for src in srcs:
        starts.append(starts[-1] + src.shape[0] // rows)
    n_blocks = starts[-1]

    def per_array(blk, fn):
        for a, (src, dst) in enumerate(zip(srcs, dsts)):
            @pl.when(jnp.logical_and(blk >= starts[a], blk < starts[a + 1]))
            def _():
                fn(src, dst, pl.ds((blk - starts[a]) * rows, rows))

    def load(src, block_rows, blk):
        return pltpu.make_async_copy(src.at[block_rows, :], f32_stage.at[blk % 2], sems.at[blk % 2])

    def store(dst, block_rows, blk):
        return pltpu.make_async_copy(bf16_stage.at[blk % 2], dst.at[block_rows, :], sems.at[2 + blk % 2])

    first_rows = pl.ds(0, rows)

    @pl.when(step == 0)
    def _():
        load(srcs[0], first_rows, step).start(priority=PANEL_DMA_PRIORITY)

    @pl.when(step + 1 < n_blocks)
    def _():
        per_array(step + 1, lambda src, dst, r: load(src, r, step + 1).start(priority=PANEL_DMA_PRIORITY))

    @pl.when(step < n_blocks)
    def _():
        load(srcs[0], first_rows, step).wait()

        @pl.when(step >= 2)
        def _():
            store(dsts[0], first_rows, step - 2).wait()
        bf16_stage[step % 2] = f32_stage[step % 2].astype(bf16_stage.dtype)
        per_array(step, lambda src, dst, r: store(dst, r, step).start(priority=PANEL_DMA_PRIORITY))

    @pl.when(step == n_blocks - 1)
    def _():
        if n_blocks > 1:
            store(dsts[0], first_rows, step - 1).wait()
        store(dsts[0], first_rows, step).wait()


def _side_cast_plan(side, side_rows, n_steps):
    if not side:
        return [], [], []
    cols = side[0].shape[1]
    assert all(s.shape[1] == cols and s.dtype == F32 and s.shape[0] % side_rows == 0 for s in side)
    assert sum(s.shape[0] // side_rows for s in side) <= n_steps
    any_specs = [pl.BlockSpec(memory_space=pl.ANY)] * len(side)
    out_shape = [jax.ShapeDtypeStruct(s.shape, BF16) for s in side]
    scratch = [((2, side_rows, cols), F32), ((2, side_rows, cols), BF16)]
    return any_specs, out_shape, scratch


def _scratch_shapes(side_scratch):
    if not side_scratch:
        return []
    return [pltpu.VMEM(s, d) for s, d in side_scratch] + [pltpu.SemaphoreType.DMA((4,))]


def _swiglu_tile(x, wg_ref, wu_ref):
    g = jnp.dot(x, wg_ref[...], preferred_element_type=F32)
    u = jnp.dot(x, wu_ref[...], preferred_element_type=F32)
    return (g * jax.nn.sigmoid(g) * u).astype(BF16)


def _ffn_up_kernel(x_ref, wg_ref, wu_ref, *rest, has_tail, n_side):
    rest = list(rest)
    wgt_ref, wut_ref = (rest.pop(0), rest.pop(0)) if has_tail else (None, None)
    side_srcs = [rest.pop(0) for _ in range(n_side)]
    o_ref = rest.pop(0)
    ot_ref = rest.pop(0) if has_tail else None
    side_dsts = [rest.pop(0) for _ in range(n_side)]
    if n_side:
        step = pl.program_id(0) * pl.num_programs(1) + pl.program_id(1)
        _side_cast_step(step, side_srcs, side_dsts, *rest)
    o_ref[...] = _swiglu_tile(x_ref[...], wg_ref, wu_ref)
    if has_tail:
        @pl.when(pl.program_id(1) == 0)
        def _():
            ot_ref[...] = _swiglu_tile(x_ref[...], wgt_ref, wut_ref)


def _ffn_up(x, w_gate, w_up, side=(), *, tm, tn, side_rows=None):
    t, k = x.shape
    n_main, n_tail = _split_tail(w_gate.shape[1], tn)
    assert t % tm == 0
    grid = (t // tm, n_main // tn)
    side_specs, side_out, side_scratch = _side_cast_plan(side, side_rows, grid[0] * grid[1])
    w_spec = pl.BlockSpec((k, tn), lambda i, j: (0, j))
    in_specs = [pl.BlockSpec((tm, k), lambda i, j: (i, 0)), w_spec, w_spec]
    out_specs = [pl.BlockSpec((tm, tn), lambda i, j: (i, j))]
    out_shape = [jax.ShapeDtypeStruct((t, n_main), BF16)]
    operands = [x, w_gate, w_up]
    blocks = [((tm, k), x.dtype), ((k, tn), BF16), ((k, tn), BF16), ((tm, tn), BF16)]
    if n_tail:
        tail_spec = pl.BlockSpec((k, n_tail), lambda i, j: (0, n_main // n_tail),
                                 pipeline_mode=pl.Buffered(1))
        in_specs += [tail_spec, tail_spec]
        out_specs.append(pl.BlockSpec((tm, n_tail), lambda i, j: (i, 0)))
        out_shape.append(jax.ShapeDtypeStruct((t, n_tail), BF16))
        operands += [w_gate, w_up]
        blocks += [((k, n_tail), BF16), ((tm, n_tail), BF16)]
    outs = pl.pallas_call(
        functools.partial(_ffn_up_kernel, has_tail=bool(n_tail), n_side=len(side)),
        grid=grid,
        in_specs=in_specs + side_specs,
        out_specs=out_specs + side_specs,
        out_shape=out_shape + side_out,
        scratch_shapes=_scratch_shapes(side_scratch),
        compiler_params=_params(("arbitrary", "arbitrary"),
                                _vmem_limit(blocks, side_scratch + [((tm, tn), F32)] * 4)),
        name="ffn_up",
    )(*operands, *side)
    n_out = 2 if n_tail else 1
    return outs[0], (outs[1] if n_tail else None), list(outs[n_out:])


def _w_in_kernel(x_ref, w_ref, vg_ref, *rest, n_u, n_v, n_side):
    side_srcs, rest = rest[:n_side], rest[n_side:]
    (u_ref, v_ref, p_ref), rest = rest[:3], rest[3:]
    side_dsts, scratch = rest[:n_side], rest[n_side:]
    j = pl.program_id(1)
    if n_side:
        _side_cast_step(pl.program_id(0) * pl.num_programs(1) + j, side_srcs, side_dsts, *scratch)

    matmul = lambda: jnp.dot(x_ref[...], w_ref[...], preferred_element_type=F32)

    @pl.when(j < n_u)
    def _():
        u_ref[...] = _gelu(matmul()).astype(u_ref.dtype)

    @pl.when(jnp.logical_and(j >= n_u, j < n_u + n_v))
    def _():
        g = _gelu(matmul())
        for hd in range(v_ref.shape[1] // HEAD_DIM):
            cols = slice(hd * HEAD_DIM, (hd + 1) * HEAD_DIM)
            gh = g[:, cols]
            v_ref[:, cols] = (gh * _rms_scale(gh) * vg_ref[:, cols]).astype(v_ref.dtype)

    @pl.when(j >= n_u + n_v)
    def _():
        p_ref[...] = matmul()


def _w_in(x, w, v_gain, side=(), *, tm, tn, side_rows=None):
    t, k = x.shape
    assert t % tm == 0 and D_A % tn == 0 and D_B % tn == 0 and w.shape[1] == 2 * D_A + D_B
    n_u = n_v = D_A // tn
    n_p = D_B // tn
    grid = (t // tm, n_u + n_v + n_p)
    clamp = lambda j, lo, cnt: jnp.clip(j - lo, 0, cnt - 1)
    side_specs, side_out, side_scratch = _side_cast_plan(side, side_rows, grid[0] * grid[1])
    blocks = [((tm, k), x.dtype), ((k, tn), BF16), ((tm, tn), BF16), ((tm, tn), BF16), ((tm, tn), F32)]
    outs = pl.pallas_call(
        functools.partial(_w_in_kernel, n_u=n_u, n_v=n_v, n_side=len(side)),
        grid=grid,
        in_specs=[pl.BlockSpec((tm, k), lambda i, j: (i, 0)),
                  pl.BlockSpec((k, tn), lambda i, j: (0, j)),
                  pl.BlockSpec((1, tn), lambda i, j: (0, clamp(j, n_u, n_v)))] + side_specs,
        out_specs=[pl.BlockSpec((tm, tn), lambda i, j: (i, clamp(j, 0, n_u))),
                   pl.BlockSpec((tm, tn), lambda i, j: (i, clamp(j, n_u, n_v))),
                   pl.BlockSpec((tm, tn), lambda i, j: (i, clamp(j, n_u + n_v, n_p)))] + side_specs,
        out_shape=[jax.ShapeDtypeStruct((t, D_A), BF16), jax.ShapeDtypeStruct((t, D_A), BF16),
                   jax.ShapeDtypeStruct((t, D_B), F32)] + side_out,
        scratch_shapes=_scratch_shapes(side_scratch),
        compiler_params=_params(("arbitrary", "arbitrary"),
                                _vmem_limit(blocks, side_scratch + [((tm, tn), F32)] * 3)),
        name="w_in",
    )(x, w, v_gain, *side)
    return outs[0], outs[1], outs[2], list(outs[3:])


EPILOGUE_ROWS = 128
STAGE_SLOTS = 4


def _down_kernel(*refs, scale, emit_residual, has_tail, chunk):
    refs = list(refs)
    xm_ref = refs.pop(0)
    xt_ref = refs.pop(0) if has_tail else None
    wm_ref = refs.pop(0)
    wt_ref = refs.pop(0) if has_tail else None
    gp_ref, gn_ref, h_hbm = refs.pop(0), refs.pop(0), refs.pop(0)
    hout_hbm = refs.pop(0) if emit_residual else None
    nxt_hbm = refs.pop(0)
    y_scr, ss_scr, h_scr = refs.pop(0), refs.pop(0), refs.pop(0)
    stage = refs.pop(0) if emit_residual else None
    (sems,) = refs

    i, k = pl.program_id(0), pl.program_id(1)
    n_i, n_k = pl.num_programs(0), pl.num_programs(1)
    tm, n_cols = y_scr.shape
    col_chunks = [slice(c * chunk, (c + 1) * chunk) for c in range(n_cols // chunk)]
    n_slots = stage.shape[0] if emit_residual else 0
    rows = lambda p: pl.ds(p * tm, tm)

    def fetch(p):
        return pltpu.make_async_copy(h_hbm.at[rows(p), :], h_scr, sems.at[0])

    def store_panel(p):
        dst = hout_hbm if emit_residual else nxt_hbm
        return pltpu.make_async_copy(h_scr, dst.at[rows(p), :], sems.at[1])

    rb = stage.shape[1] if emit_residual else EPILOGUE_ROWS
    row_blocks = [slice(b * rb, (b + 1) * rb) for b in range(tm // rb)]

    def store_stage(p, b):
        slot = b % n_slots
        return pltpu.make_async_copy(stage.at[slot], nxt_hbm.at[pl.ds(p * tm + b * rb, rb), :],
                                     sems.at[2 + slot])

    def outstanding(p):
        last = range(len(row_blocks) - n_slots, len(row_blocks))
        return [store_panel(p)] + [store_stage(p, b) for b in last]

    def partial_y(cols):
        return jnp.dot(xm_ref[...], wm_ref[:, cols], preferred_element_type=F32)

    def first_y(cols):
        y = partial_y(cols)
        if has_tail:
            y = y + jnp.dot(xt_ref[...], wt_ref[:, cols], preferred_element_type=F32)
        return y

    def row_scale(ss):
        return lax.rsqrt(jnp.sum(ss, axis=-1, keepdims=True) * (1.0 / n_cols) + EPS)

    def epilogue_rows(p, b):
        rws = row_blocks[b]
        r1 = scale * row_scale(ss_scr[rws, :])
        sq = jnp.zeros((rb, V7X_LANES), F32)
        for cols in col_chunks:
            h_new = h_scr[rws, cols] + y_scr[rws, cols] * r1 * gp_ref[:, cols]
            h_scr[rws, cols] = h_new
            sq = sq + _lane_fold(h_new * h_new)
        r2 = row_scale(sq)
        if emit_residual and b >= n_slots:
            store_stage(p, b - n_slots).wait()
        for cols in col_chunks:
            nxt = h_scr[rws, cols] * r2 * gn_ref[:, cols]
            if emit_residual:
                stage[b % n_slots, :, cols] = nxt.astype(stage.dtype)
            else:
                h_scr[rws, cols] = nxt
        if emit_residual:
            store_stage(p, b).start(priority=PANEL_DMA_PRIORITY)

    has_prev = i > 0

    @pl.when(k == 0)
    def _():
        for cols in col_chunks:
            y_scr[:, cols] = first_y(cols)

    @pl.when(jnp.logical_and(k > 0, k < n_k - 1))
    def _():
        for cols in col_chunks:
            y_scr[:, cols] += partial_y(cols)

    @pl.when(k == n_k // 2)
    def _():
        @pl.when(has_prev)
        def _():
            for copy in outstanding(i - 1):
                copy.wait()
        fetch(i).start(priority=PANEL_DMA_PRIORITY)

    @pl.when(k == n_k - 1)
    def _():
        sq = jnp.zeros((tm, V7X_LANES), F32)
        for cols in col_chunks:
            y = y_scr[:, cols] + partial_y(cols)
            y_scr[:, cols] = y
            sq = sq + _lane_fold(y * y)
        ss_scr[...] = sq

    @pl.when(k == n_k - 1)
    def _():
        fetch(i).wait()
        for b in range(len(row_blocks)):
            epilogue_rows(i, b)
        store_panel(i).start(priority=PANEL_DMA_PRIORITY)

        @pl.when(i == n_i - 1)
        def _():
            for copy in outstanding(i):
                copy.wait()


def _down_residual(x_main, x_tail, w, h, gain_post, gain_next, *, scale, emit_residual,
                   tm, tk, chunk, name):
    if x_main.ndim == 3:
        n_k, t, _ = x_main.shape
        assert x_main.shape[2] == tk
        k_main = n_k * tk
        x_spec = pl.BlockSpec((None, tm, tk), lambda i, k: (k, i, 0))
    else:
        t, k_main = x_main.shape
        n_k = k_main // tk
        x_spec = pl.BlockSpec((tm, tk), lambda i, k: (i, k))
    k_tail = 0 if x_tail is None else x_tail.shape[1]
    n = w.shape[1]
    assert t % tm == 0 and tm % EPILOGUE_ROWS == 0 and k_main % tk == 0 and n % chunk == 0
    assert n_k >= 4 and h.shape == (t, n) and w.shape[0] == k_main + k_tail
    assert k_tail == 0 or k_main % k_tail == 0

    any_spec = pl.BlockSpec(memory_space=pl.ANY)
    once = dict(pipeline_mode=pl.Buffered(1))
    gain_spec = pl.BlockSpec((1, n), lambda i, k: (0, 0), **once)
    in_specs = [x_spec]
    operands = [x_main]
    blocks = [((tm, tk), BF16), ((tk, n), BF16)]
    if k_tail:
        in_specs.append(pl.BlockSpec((tm, k_tail), lambda i, k: (i, 0)))
        operands.append(x_tail)
        blocks.append(((tm, k_tail), BF16))
    in_specs.append(pl.BlockSpec((tk, n), lambda i, k: (k, 0)))
    operands.append(w)
    if k_tail:
        in_specs.append(pl.BlockSpec((k_tail, n), lambda i, k: (k_main // k_tail, 0), **once))
        operands.append(w)
    in_specs += [gain_spec, gain_spec, any_spec]
    operands += [gain_post, gain_next, h]

    out_shape = [jax.ShapeDtypeStruct((t, n), BF16 if emit_residual else F32)]
    scratch = [((tm, n), F32), ((tm, V7X_LANES), F32), ((tm, n), F32)]
    n_sems = 2
    if emit_residual:
        out_shape.insert(0, jax.ShapeDtypeStruct((t, n), F32))
        n_slots = min(STAGE_SLOTS, tm // EPILOGUE_ROWS)
        scratch.append(((n_slots, EPILOGUE_ROWS, n), BF16))
        n_sems = 2 + n_slots
    single = scratch + [((k_tail, n), BF16), ((2, n), F32)]
    return pl.pallas_call(
        functools.partial(_down_kernel, scale=scale, emit_residual=emit_residual,
                          has_tail=bool(k_tail), chunk=chunk),
        grid=(t // tm, n_k),
        in_specs=in_specs,
        out_specs=[any_spec] * len(out_shape),
        out_shape=out_shape,
        scratch_shapes=[pltpu.VMEM(s, d) for s, d in scratch] + [pltpu.SemaphoreType.DMA((n_sems,))],
        compiler_params=_params(("arbitrary", "arbitrary"), _vmem_limit(blocks, single)),
        name=name,
    )(*operands)


def _mixer_kernel(u_ref, v_ref, p_ref, pprev_ref, pnext_ref, ws_ref, bs_ref, wp_ref, ps_ref,
                  o_ref, *, tiles_per_seq, seq_len):
    tm = u_ref.shape[0]
    n_chunks = tm // CHUNK
    panel = o_ref.shape[2]

    def put(rows, col, value):
        off = col % panel
        o_ref[col // panel, rows, off:off + value.shape[1]] = value.astype(o_ref.dtype)

    for hd in range(N_HEADS):
        cols = slice(hd * HEAD_DIM, (hd + 1) * HEAD_DIM)
        v_h = jnp.concatenate(
            [v_ref[c * CHUNK:(c + 1) * CHUNK, cols] for c in range(n_chunks)], axis=1)
        sv = jnp.dot(ws_ref[hd], v_h, preferred_element_type=F32)
        bias = bs_ref[hd]
        for c in range(n_chunks):
            rows = slice(c * CHUNK, (c + 1) * CHUNK)
            gate = sv[:, c * HEAD_DIM:(c + 1) * HEAD_DIM] + bias
            put(rows, hd * HEAD_DIM, u_ref[rows, cols].astype(F32) * gate)

    pos = pl.program_id(0) % tiles_per_seq
    prev = jnp.where(pos > 0, pprev_ref[...], 0.0)
    nxt = jnp.where(pos < tiles_per_seq - 1, pnext_ref[...], 0.0)
    row = lax.broadcasted_iota(jnp.int32, (tm, 1), 0) + pos * tm
    for g, w in enumerate(POOL_WINDOWS):
        half = w // 2
        cols = slice(g * D_POOL_GROUP, (g + 1) * D_POOL_GROUP)
        p = p_ref[:, cols]
        s = jnp.concatenate([prev[:, cols], p, nxt[:, cols]], axis=0)
        step = 1
        while step < w:
            s = s[:s.shape[0] - step] + s[step:]
            step *= 2
        win = s[POOL_HALO - half:POOL_HALO - half + tm]
        count = jnp.minimum(row + half, seq_len) - jnp.maximum(row - half, 0)
        pooled = win * (1.0 / count.astype(F32)) - p
        y_b = jnp.dot(pooled.astype(BF16), wp_ref[g], preferred_element_type=F32) * ps_ref[:, cols]
        put(slice(None), D_A + g * D_POOL_GROUP, y_b)


def _mixer(u, v, p, w_spatial, b_spatial, w_pool, pool_scale, *, seq_len, panel, tm=512):
    t = u.shape[0]
    assert seq_len % tm == 0 and tm % CHUNK == 0 and panel % HEAD_DIM == 0 and panel % D_POOL_GROUP == 0
    halo_blocks_per_tile = tm // POOL_HALO
    n_halo_blocks = t // POOL_HALO
    row_spec = lambda width: pl.BlockSpec((tm, width), lambda i: (i, 0))
    const3 = lambda shape: pl.BlockSpec(shape, lambda i: (0, 0, 0))
    blocks = [((tm, D_A), BF16)] * 2 + [((tm, D_B), F32), ((tm, D_MODEL), BF16),
              (w_spatial.shape, BF16), (b_spatial.shape, F32), (w_pool.shape, BF16)]
    return pl.pallas_call(
        functools.partial(_mixer_kernel, tiles_per_seq=seq_len // tm, seq_len=seq_len),
        grid=(t // tm,),
        in_specs=[
            row_spec(D_A), row_spec(D_A), row_spec(D_B),
            pl.BlockSpec((POOL_HALO, D_B),
                         lambda i: (jnp.maximum(i * halo_blocks_per_tile - 1, 0), 0)),
            pl.BlockSpec((POOL_HALO, D_B),
                         lambda i: (jnp.minimum((i + 1) * halo_blocks_per_tile, n_halo_blocks - 1), 0)),
            const3(w_spatial.shape), const3(b_spatial.shape), const3(w_pool.shape),
            pl.BlockSpec((1, D_B), lambda i: (0, 0)),
        ],
        out_specs=pl.BlockSpec((D_MODEL // panel, tm, panel), lambda i: (0, i, 0)),
        out_shape=jax.ShapeDtypeStruct((D_MODEL // panel, t, panel), BF16),
        compiler_params=_params(("parallel",),
                                _vmem_limit(blocks, [((tm + 2 * POOL_HALO, D_B), F32)] * 3)),
        name="token_mixer",
    )(u, v, p, p, p, w_spatial, b_spatial, w_pool, pool_scale)


W_OUT_TK = 1024
_FFN_UP = functools.partial(_ffn_up, tm=1024, tn=512)
_FFN2_UP = functools.partial(_ffn_up, tm=1024, tn=768)
_FFN_DOWN = functools.partial(_down_residual, tm=1024, tk=768, chunk=512)
_W_OUT = functools.partial(_down_residual, tm=1024, tk=W_OUT_TK, chunk=512)
_W_IN = functools.partial(_w_in, tm=1024, tn=512)


def _as_rows(x):
    b, s, d = x.shape
    return x.reshape(b * s, d)


def _mix_out(u, v, p, h1, w, seq_len):
    y_cat = _mixer(u, v, p, w["w_spatial"], w["b_spatial"], w["w_pool"], w["pool_scale"],
                   seq_len=seq_len, panel=W_OUT_TK)
    return _W_OUT(y_cat, None, w["w_out"], h1, w["mix_post"], w["ffn2_pre"],
                  scale=1.0, emit_residual=True, name="w_out")


def _ffn2(h2, xn, w, out_shape):
    act, act_tail, _ = _FFN2_UP(xn, w["ffn2_gate"], w["ffn2_up"])
    (out,) = _FFN_DOWN(act, act_tail, w["ffn2_down"], h2, w["ffn2_post"], w["final_norm"],
                       scale=FFN_RESIDUAL, emit_residual=False, name="ffn2_down")
    return out.reshape(out_shape)


def kernel(x_prompt, x_sample, ffn1_pre, ffn1_post, ffn1_gate, ffn1_up, ffn1_down, mix_pre, mix_post, w_in, w_spatial, b_spatial, v_norm, w_pool, pool_scale, w_out, ffn2_pre, ffn2_post, ffn2_gate, ffn2_up, ffn2_down, final_norm):
    assert ffn1_pre.shape[0] == 1, "single-layer trunk"
    row = lambda g: g[0].reshape(1, -1).astype(F32)
    mxu = lambda m: m[0].astype(BF16)
    raw = lambda m: m[0].astype(F32)
    w = {
        "ffn1_pre": row(ffn1_pre), "ffn1_post": row(ffn1_post),
        "ffn1_gate": mxu(ffn1_gate), "ffn1_up": mxu(ffn1_up),
        "mix_pre": row(mix_pre), "mix_post": row(mix_post),
        "w_spatial": mxu(w_spatial),
        "b_spatial": jnp.broadcast_to(b_spatial[0][:, :, None], (N_HEADS, CHUNK, HEAD_DIM)).astype(F32),
        "v_norm": row(v_norm), "w_pool": mxu(w_pool), "pool_scale": row(pool_scale),
        "ffn2_pre": row(ffn2_pre), "ffn2_post": row(ffn2_post),
        "final_norm": row(final_norm),
    }
    h0_p, h0_s = _as_rows(x_prompt), _as_rows(x_sample)
    act_p, tail_p, (w["ffn1_down"],) = _FFN_UP(
        _rmsnorm_rows(h0_p, w["ffn1_pre"]), w["ffn1_gate"], w["ffn1_up"], (raw(ffn1_down),),
        side_rows=FFN1_DOWN_CAST_ROWS)
    act_s, tail_s, (w["w_in"],) = _FFN_UP(
        _rmsnorm_rows(h0_s, w["ffn1_pre"]), w["ffn1_gate"], w["ffn1_up"], (raw(w_in),),
        side_rows=W_IN_CAST_ROWS)
    ffn1_down_args = dict(scale=FFN_RESIDUAL, emit_residual=True, name="ffn1_down")
    h1_p, a_p = _FFN_DOWN(act_p, tail_p, w["ffn1_down"], h0_p, w["ffn1_post"], w["mix_pre"], **ffn1_down_args)
    h1_s, a_s = _FFN_DOWN(act_s, tail_s, w["ffn1_down"], h0_s, w["ffn1_post"], w["mix_pre"], **ffn1_down_args)

    u_s, v_s, p_s, (w["ffn2_down"], w["w_out"]) = _W_IN(
        a_s, w["w_in"], w["v_norm"], (raw(ffn2_down), raw(w_out)), side_rows=SQUARE_CAST_ROWS)
    u_p, v_p, p_p, (w["ffn2_gate"], w["ffn2_up"]) = _W_IN(
        a_p, w["w_in"], w["v_norm"], (raw(ffn2_gate), raw(ffn2_up)), side_rows=FFN_UP_CAST_ROWS)
    h2_p, xn_p = _mix_out(u_p, v_p, p_p, h1_p, w, x_prompt.shape[1])
    h2_s, xn_s = _mix_out(u_s, v_s, p_s, h1_s, w, x_sample.shape[1])
    return _ffn2(h2_p, xn_p, w, x_prompt.shape), _ffn2(h2_s, xn_s, w, x_sample.shape)
```
